```python
import math
import jax, jax.numpy as jnp
from jax import lax
import numpy as np

D_MODEL = 1024
BATCH = 32
SEQ = 2048
DEPTH = 2
DEC_BATCH = 1
DEC_SEQ = 16384
PAST_LEN = 128

HEAD_DIM = 64
N_Q_HEADS = 12
N_KV_HEADS_A = 4
N_MEM_HEADS = 4
MEM_LEN = 256
WINDOW = 128
BLOCK = 128
GRID_W = 64
NA_ROWS_MAX = 8
NA_COLS = 16
D_FF = 2816
CONV_W = 3
EPS = 1e-6
NEG = -1e30
N_LAYERS_A = (DEPTH + 1) // 2
N_LAYERS_B = DEPTH // 2
Q_WIDTH = N_Q_HEADS * HEAD_DIM
KV_WIDTH_A = N_KV_HEADS_A * HEAD_DIM
MEM_WIDTH = N_MEM_HEADS * HEAD_DIM
IN_WIDTH_A = Q_WIDTH + 2 * KV_WIDTH_A + MEM_WIDTH
IN_WIDTH_B = 3 * Q_WIDTH + MEM_WIDTH
MIX_WIDTH = Q_WIDTH + MEM_WIDTH
RPB_ROWS = 2 * NA_ROWS_MAX - 1
RPB_COLS = 2 * NA_COLS - 1

kernel_name = "hybrid_window_gqa_natten_memory_convffn_encoder"


def rms_norm(x, g):
    xf = x.astype(jnp.float32)
    y = xf * lax.rsqrt(jnp.mean(xf * xf, axis=-1, keepdims=True) + EPS)
    return (y * g.astype(jnp.float32)).astype(x.dtype)


def alibi_slopes(n):
    return jnp.asarray((2.0 ** (-8.0 * np.arange(1, n + 1, dtype=np.float32) / n)).astype(np.float32))


def window_gqa(q, k, v, sink):
    B, T, Hq, d = q.shape
    Hkv = k.shape[2]
    rep = Hq // Hkv
    nb = T // BLOCK
    kb_len = BLOCK + 2 * WINDOW
    kp = jnp.pad(k, ((0, 0), (WINDOW, WINDOW), (0, 0), (0, 0)))
    vp = jnp.pad(v, ((0, 0), (WINDOW, WINDOW), (0, 0), (0, 0)))
    qg = q.reshape(B, T, Hkv, rep, d)
    rel = jnp.arange(BLOCK)[:, None] - jnp.arange(kb_len)[None, :] + WINDOW
    in_window = jnp.abs(rel) <= WINDOW
    alibi = -alibi_slopes(Hq).reshape(Hkv, rep, 1, 1) * jnp.abs(rel).astype(jnp.float32)
    sink_f = sink.astype(jnp.float32).reshape(Hkv, rep, 1)
    scale = HEAD_DIM ** -0.5

    def one_block(j):
        start = j * BLOCK
        qb = lax.dynamic_slice_in_dim(qg, start, BLOCK, axis=1).astype(jnp.float32)
        kb = lax.dynamic_slice_in_dim(kp, start, kb_len, axis=1).astype(jnp.float32)
        vb = lax.dynamic_slice_in_dim(vp, start, kb_len, axis=1).astype(jnp.float32)
        s_pos = start - WINDOW + jnp.arange(kb_len)
        valid = in_window & ((s_pos >= 0) & (s_pos < T))[None, :]
        sc = jnp.einsum('bqgrd,bkgd->bgrqk', qb, kb) * scale + alibi
        sc = jnp.where(valid, sc, NEG)
        m = jnp.maximum(sc.max(axis=-1), sink_f)
        p = jnp.exp(sc - m[..., None])
        denom = p.sum(axis=-1) + jnp.exp(sink_f - m)
        o = jnp.einsum('bgrqk,bkgd->bqgrd', p, vb)
        o = o / jnp.transpose(denom, (0, 3, 1, 2))[..., None]
        return o.astype(q.dtype)

    out = lax.map(one_block, jnp.arange(nb))
    return jnp.moveaxis(out, 0, 1).reshape(B, T, Hq * d)


def neighbourhood_attn(q, k, v, rpb):
    B, T, H, d = q.shape
    rows = T // GRID_W
    kr = min(NA_ROWS_MAX, rows)
    qg = q.reshape(B, rows, GRID_W, H, d)
    kg = k.reshape(B, rows, GRID_W, H, d)
    vg = v.reshape(B, rows, GRID_W, H, d)
    cols = np.arange(GRID_W)
    c0 = np.clip(cols - NA_COLS // 2, 0, GRID_W - NA_COLS)
    col_idx = c0[:, None] + np.arange(NA_COLS)[None, :]
    dc = col_idx - cols[:, None]
    bias_cols = rpb.astype(jnp.float32)[:, :, dc + NA_COLS - 1]
    scale = HEAD_DIM ** -0.5

    def one_row(r):
        r0 = jnp.clip(r - kr // 2, 0, rows - kr)
        qr = lax.dynamic_index_in_dim(qg, r, axis=1, keepdims=False).astype(jnp.float32)
        kw = lax.dynamic_slice_in_dim(kg, r0, kr, axis=1)
        vw = lax.dynamic_slice_in_dim(vg, r0, kr, axis=1)
        kq = kw[:, :, col_idx].astype(jnp.float32)
        vq = vw[:, :, col_idx].astype(jnp.float32)
        dr = r0 + jnp.arange(kr) - r + NA_ROWS_MAX - 1
        bias = jnp.transpose(bias_cols[:, dr], (0, 2, 1, 3))
        sc = jnp.einsum('bchd,bicjhd->bhcij', qr, kq) * scale + bias[None]
        p = jax.nn.softmax(sc.reshape(B, H, GRID_W, kr * NA_COLS), axis=-1)
        p = p.reshape(B, H, GRID_W, kr, NA_COLS)
        o = jnp.einsum('bhcij,bicjhd->bchd', p, vq)
        return o.astype(q.dtype)

    out = lax.map(one_row, jnp.arange(rows))
    return jnp.moveaxis(out, 0, 1).reshape(B, T, H * d)


def memory_attn(qm, mem_n, w_mem_kv):
    B, T, Hm, d = qm.shape
    kv = mem_n @ w_mem_kv
    km, vm = jnp.split(kv, 2, axis=-1)
    km = km.reshape(B, -1, Hm, d).astype(jnp.float32)
    vm = vm.reshape(B, -1, Hm, d).astype(jnp.float32)
    sc = jnp.einsum('bthd,bmhd->bhtm', qm.astype(jnp.float32), km) * (HEAD_DIM ** -0.5)
    p = jax.nn.softmax(sc, axis=-1)
    o = jnp.einsum('bhtm,bmhd->bthd', p, vm)
    return o.reshape(B, T, Hm * d).astype(qm.dtype)


def conv_ffn(h, w_gate, w_up, conv_w, conv_b, w_down):
    T = h.shape[1]
    g = h @ w_gate
    pad = CONV_W // 2
    gp = jnp.pad(g, ((0, 0), (pad, pad), (0, 0)))
    g = sum(gp[:, tap:tap + T] * conv_w[tap] for tap in range(CONV_W)) + conv_b
    return (jax.nn.silu(g) * (h @ w_up)) @ w_down


def trunk(x, mem, g_mix, g_mem, w_in_a, sink_a, w_in_b, rpb_b, w_mem_kv, w_o,
          g_ffn, w_gate, w_up, conv_w, conv_b, w_down, g_final):
    B, T, _ = x.shape
    for i in range(DEPTH):
        h = rms_norm(x, g_mix[i])
        mem_n = rms_norm(mem, g_mem[i])
        j = i // 2
        if i % 2 == 0:
            proj = h @ w_in_a[j]
            q, k, v, qm = jnp.split(proj, [Q_WIDTH, Q_WIDTH + KV_WIDTH_A, Q_WIDTH + 2 * KV_WIDTH_A], axis=-1)
            o_mix = window_gqa(q.reshape(B, T, N_Q_HEADS, HEAD_DIM),
                               k.reshape(B, T, N_KV_HEADS_A, HEAD_DIM),
                               v.reshape(B, T, N_KV_HEADS_A, HEAD_DIM), sink_a[j])
        else:
            proj = h @ w_in_b[j]
            q, k, v, qm = jnp.split(proj, [Q_WIDTH, 2 * Q_WIDTH, 3 * Q_WIDTH], axis=-1)
            o_mix = neighbourhood_attn(q.reshape(B, T, N_Q_HEADS, HEAD_DIM),
                                       k.reshape(B, T, N_Q_HEADS, HEAD_DIM),
                                       v.reshape(B, T, N_Q_HEADS, HEAD_DIM), rpb_b[j])
        o_mem = memory_attn(qm.reshape(B, T, N_MEM_HEADS, HEAD_DIM), mem_n, w_mem_kv[i])
        x = x + jnp.concatenate([o_mix, o_mem], axis=-1) @ w_o[i]
        h = rms_norm(x, g_ffn[i])
        x = x + conv_ffn(h, w_gate[i], w_up[i], conv_w[i], conv_b[i], w_down[i])
    return rms_norm(x, g_final)


def setup_inputs(seed: int = 0) -> dict:
    key = jax.random.key(seed)
    ks = jax.random.split(key, 20)
    f32 = jnp.float32

    def nrm(k, shape, scale):
        return jax.random.normal(k, shape, f32) * scale

    return {
        "x_prompt": nrm(ks[0], (BATCH, SEQ, D_MODEL), 1.0),
        "x_sample": nrm(ks[1], (DEC_BATCH, DEC_SEQ, D_MODEL), 1.0),
        "mem_prompt": nrm(ks[2], (BATCH, MEM_LEN, D_MODEL), 1.0),
        "mem_sample": nrm(ks[3], (DEC_BATCH, MEM_LEN, D_MODEL), 1.0),
        "g_mix": 1.0 + nrm(ks[4], (DEPTH, D_MODEL), 0.02),
        "g_mem": 1.0 + nrm(ks[5], (DEPTH, D_MODEL), 0.02),
        "w_in_a": nrm(ks[6], (N_LAYERS_A, D_MODEL, IN_WIDTH_A), D_MODEL ** -0.5),
        "sink_a": nrm(ks[7], (N_LAYERS_A, N_Q_HEADS), 0.5),
        "w_in_b": nrm(ks[8], (N_LAYERS_B, D_MODEL, IN_WIDTH_B), D_MODEL ** -0.5),
        "rpb_b": nrm(ks[9], (N_LAYERS_B, N_Q_HEADS, RPB_ROWS, RPB_COLS), 0.1),
        "w_mem_kv": nrm(ks[10], (DEPTH, D_MODEL, 2 * MEM_WIDTH), D_MODEL ** -0.5),
        "w_o": nrm(ks[11], (DEPTH, MIX_WIDTH, D_MODEL), MIX_WIDTH ** -0.5),
        "g_ffn": 1.0 + nrm(ks[12], (DEPTH, D_MODEL), 0.02),
        "w_gate": nrm(ks[13], (DEPTH, D_MODEL, D_FF), D_MODEL ** -0.5),
        "w_up": nrm(ks[14], (DEPTH, D_MODEL, D_FF), D_MODEL ** -0.5),
        "conv_w": nrm(ks[15], (DEPTH, CONV_W, D_FF), CONV_W ** -0.5),
        "conv_b": nrm(ks[16], (DEPTH, D_FF), 0.01),
        "w_down": nrm(ks[17], (DEPTH, D_FF, D_MODEL), D_FF ** -0.5),
        "g_final": 1.0 + nrm(ks[18], (D_MODEL,), 0.02),
    }


def reference(x_prompt, x_sample, mem_prompt, mem_sample, g_mix, g_mem, w_in_a, sink_a, w_in_b, rpb_b,
              w_mem_kv, w_o, g_ffn, w_gate, w_up, conv_w, conv_b, w_down, g_final):
    y_prompt = trunk(x_prompt, mem_prompt, g_mix, g_mem, w_in_a, sink_a, w_in_b, rpb_b, w_mem_kv, w_o,
                     g_ffn, w_gate, w_up, conv_w, conv_b, w_down, g_final)
    y_sample = trunk(x_sample, mem_sample, g_mix, g_mem, w_in_a, sink_a, w_in_b, rpb_b, w_mem_kv, w_o,
                     g_ffn, w_gate, w_up, conv_w, conv_b, w_down, g_final)
    return (y_prompt, y_sample)
```

```python
import functools

import numpy as np
import jax
import jax.numpy as jnp
from jax import lax
from jax.experimental import pallas as pl
from jax.experimental.pallas import tpu as pltpu

D_MODEL = 1024
DEPTH = 2
HEAD_DIM = 64
N_Q_HEADS = 12
N_KV_HEADS_A = 4
N_MEM_HEADS = 4
MEM_LEN = 256
WINDOW = 128
GRID_W = 64
NA_ROWS = 8
NA_COLS = 16
D_FF = 2816
CONV_W = 3
EPS = 1e-6
NEG = -1e30
Q_WIDTH = N_Q_HEADS * HEAD_DIM
KV_WIDTH_A = N_KV_HEADS_A * HEAD_DIM
MEM_WIDTH = N_MEM_HEADS * HEAD_DIM
MIX_WIDTH = Q_WIDTH + MEM_WIDTH
SCALE = HEAD_DIM ** -0.5

F32 = jnp.float32
BF16 = jnp.bfloat16

VMEM_LIMIT_BYTES = 56 * 1024 * 1024
FF_CHUNK = 256
HALO = 16

_NT = (((1,), (1,)), ((), ()))


def _params(*sem):
    return pltpu.CompilerParams(dimension_semantics=sem, vmem_limit_bytes=VMEM_LIMIT_BYTES)


def _resident(shape, index_map):
    return pl.BlockSpec(shape, index_map, pipeline_mode=pl.Buffered(1))


def _rms(x, g):
    return x * lax.rsqrt(jnp.mean(x * x, axis=-1, keepdims=True) + EPS) * g


def _norm_matmul_kernel(x_ref, g_ref, w_ref, o_ref):
    h = _rms(x_ref[...], g_ref[...]).astype(BF16)
    o_ref[...] = jnp.dot(h, w_ref[...], preferred_element_type=F32).astype(o_ref.dtype)


def _norm_matmul(x, g, w, tm):
    n, d = x.shape
    wdt = w.shape[1]
    return pl.pallas_call(
        _norm_matmul_kernel,
        grid=(n // tm,),
        in_specs=[
            pl.BlockSpec((tm, d), lambda i: (i, 0)),
            _resident((1, d), lambda i: (0, 0)),
            _resident((d, wdt), lambda i: (0, 0)),
        ],
        out_specs=pl.BlockSpec((tm, wdt), lambda i: (i, 0)),
        out_shape=jax.ShapeDtypeStruct((n, wdt), BF16),
        compiler_params=_params("parallel"),
        name="norm_matmul",
    )(x, g.reshape(1, d), w)


def _memory_attention(qm, kvm, o_ref):
    outs = []
    for h in range(N_MEM_HEADS):
        q = qm[:, h * HEAD_DIM:(h + 1) * HEAD_DIM]
        k = kvm[:, h * HEAD_DIM:(h + 1) * HEAD_DIM]
        v = kvm[:, MEM_WIDTH + h * HEAD_DIM:MEM_WIDTH + (h + 1) * HEAD_DIM]
        s = lax.dot_general(q, k, _NT, preferred_element_type=F32)
        p = jnp.exp(s - jnp.max(s, axis=-1, keepdims=True))
        den = jnp.sum(p, axis=-1, keepdims=True)
        outs.append(jnp.dot(p.astype(BF16), v, preferred_element_type=F32) / den)
    for j in range(N_MEM_HEADS // 2):
        pair = jnp.concatenate([outs[2 * j], outs[2 * j + 1]], axis=-1)
        o_ref[0, :, Q_WIDTH + 2 * j * HEAD_DIM:Q_WIDTH + (2 * j + 2) * HEAD_DIM] = pair.astype(o_ref.dtype)


def _alibi_slopes(n):
    return [float(v) for v in (2.0 ** (-8.0 * np.arange(1, n + 1, dtype=np.float32) / n)).astype(np.float32)]


def _window_kernel(sink_ref, q_ref, kp_ref, kc_ref, kn_ref, vp_ref, vc_ref, vn_ref, qm_ref, kvm_ref,
                   o_ref, kbuf, vbuf, *, tq, seq_len):
    i = pl.program_id(1)
    blk = WINDOW
    rep = N_Q_HEADS // N_KV_HEADS_A
    kbuf[0:blk] = kp_ref[0]
    kbuf[blk:blk + tq] = kc_ref[0]
    kbuf[blk + tq:] = kn_ref[0]
    vbuf[0:blk] = vp_ref[0]
    vbuf[blk:blk + tq] = vc_ref[0]
    vbuf[blk + tq:] = vn_ref[0]

    slopes = _alibi_slopes(N_Q_HEADS)
    kb_len = 3 * blk
    rows = rep * blk
    qq = lax.broadcasted_iota(jnp.int32, (rows, kb_len), 0) % blk
    kk = lax.broadcasted_iota(jnp.int32, (rows, kb_len), 1)
    rel = qq - kk + WINDOW
    arel = jnp.abs(rel)
    arel_f = arel.astype(F32)
    row_rep = lax.broadcasted_iota(jnp.int32, (rows, 1), 0) // blk

    for j in range(tq // blk):
        s_pos = i * tq + j * blk - WINDOW + kk
        valid = (arel <= WINDOW) & (s_pos >= 0) & (s_pos < seq_len)
        q_j = q_ref[0, j * blk:(j + 1) * blk, :] * SCALE
        outs = []
        for g in range(N_KV_HEADS_A):
            qg = jnp.concatenate(
                [q_j[:, (rep * g + r) * HEAD_DIM:(rep * g + r + 1) * HEAD_DIM] for r in range(rep)], axis=0)
            kg = kbuf[j * blk:j * blk + kb_len, g * HEAD_DIM:(g + 1) * HEAD_DIM]
            vg = vbuf[j * blk:j * blk + kb_len, g * HEAD_DIM:(g + 1) * HEAD_DIM]
            slope = jnp.full((rows, 1), slopes[rep * g], F32)
            sink = jnp.full((rows, 1), sink_ref[rep * g], F32)
            for r in range(1, rep):
                slope = jnp.where(row_rep == r, slopes[rep * g + r], slope)
                sink = jnp.where(row_rep == r, sink_ref[rep * g + r], sink)
            sc = lax.dot_general(qg, kg, _NT, preferred_element_type=F32) - slope * arel_f
            sc = jnp.where(valid, sc, NEG)
            m = jnp.maximum(jnp.max(sc, axis=-1, keepdims=True), sink)
            p = jnp.exp(sc - m)
            den = jnp.sum(p, axis=-1, keepdims=True) + jnp.exp(sink - m)
            o = jnp.dot(p.astype(BF16), vg, preferred_element_type=F32) / den
            outs.extend(o[r * blk:(r + 1) * blk] for r in range(rep))
        for h2 in range(N_Q_HEADS // 2):
            pair = jnp.concatenate([outs[2 * h2], outs[2 * h2 + 1]], axis=-1)
            o_ref[0, j * blk:(j + 1) * blk, 2 * h2 * HEAD_DIM:(2 * h2 + 2) * HEAD_DIM] = pair.astype(o_ref.dtype)

    _memory_attention(qm_ref[0] * SCALE, kvm_ref[0], o_ref)


def _window_attention(proj, kvm, sink, tq):
    b, t, _ = proj.shape
    blk = WINDOW
    n_blk = t // blk
    per = tq // blk
    kcol = Q_WIDTH // KV_WIDTH_A
    prev_map = lambda col: (lambda bi, i: (bi, jnp.maximum(i * per - 1, 0), col))
    cur_map = lambda col: (lambda bi, i: (bi, i, col))
    next_map = lambda col: (lambda bi, i: (bi, jnp.minimum((i + 1) * per, n_blk - 1), col))
    kernel = functools.partial(_window_kernel, tq=tq, seq_len=t)
    return pl.pallas_call(
        kernel,
        grid=(b, t // tq),
        in_specs=[
            pl.BlockSpec(memory_space=pltpu.SMEM),
            pl.BlockSpec((1, tq, Q_WIDTH), lambda bi, i: (bi, i, 0)),
            pl.BlockSpec((1, blk, KV_WIDTH_A), prev_map(kcol)),
            pl.BlockSpec((1, tq, KV_WIDTH_A), cur_map(kcol)),
            pl.BlockSpec((1, blk, KV_WIDTH_A), next_map(kcol)),
            pl.BlockSpec((1, blk, KV_WIDTH_A), prev_map(kcol + 1)),
            pl.BlockSpec((1, tq, KV_WIDTH_A), cur_map(kcol + 1)),
            pl.BlockSpec((1, blk, KV_WIDTH_A), next_map(kcol + 1)),
            pl.BlockSpec((1, tq, MEM_WIDTH), cur_map(kcol + 2)),
            pl.BlockSpec((1, MEM_LEN, 2 * MEM_WIDTH), lambda bi, i: (bi, 0, 0)),
        ],
        out_specs=pl.BlockSpec((1, tq, MIX_WIDTH), lambda bi, i: (bi, i, 0)),
        out_shape=jax.ShapeDtypeStruct((b, t, MIX_WIDTH), BF16),
        scratch_shapes=[pltpu.VMEM((tq + 2 * blk, KV_WIDTH_A), BF16),
                        pltpu.VMEM((tq + 2 * blk, KV_WIDTH_A), BF16)],
        compiler_params=_params("parallel", "parallel"),
        name="window_attention",
    )(sink, proj, proj, proj, proj, proj, proj, proj, proj, kvm)


NB_ROWS_PER_TILE = 4
NB_TQ = NB_ROWS_PER_TILE * GRID_W


def _nbr_bias_table(rpb):
    qi = np.arange(NB_TQ)
    ki = np.arange(3 * NB_TQ)
    dr = (ki[None, :] // GRID_W - NB_ROWS_PER_TILE) - qi[:, None] // GRID_W
    dc = ki[None, :] % GRID_W - qi[:, None] % GRID_W
    ri = np.clip(dr + NA_ROWS - 1, 0, 2 * NA_ROWS - 2)
    ci = np.clip(dc + NA_COLS - 1, 0, 2 * NA_COLS - 2)
    return rpb.astype(F32)[:, ri, ci]


def _nbr_kernel(q_ref, kp_ref, kc_ref, kn_ref, vp_ref, vc_ref, vn_ref, qm_ref, kvm_ref, bias_ref,
                o_ref, *, n_rows):
    i = pl.program_id(1)
    tq = NB_TQ
    nk = 3 * tq
    qi = lax.broadcasted_iota(jnp.int32, (tq, nk), 0)
    ki = lax.broadcasted_iota(jnp.int32, (tq, nk), 1)
    q_row = i * NB_ROWS_PER_TILE + qi // GRID_W
    k_row = (i - 1) * NB_ROWS_PER_TILE + ki // GRID_W
    r0 = jnp.clip(q_row - NA_ROWS // 2, 0, n_rows - NA_ROWS)
    c0 = jnp.clip(qi % GRID_W - NA_COLS // 2, 0, GRID_W - NA_COLS)
    k_col = ki % GRID_W
    valid = (k_row >= r0) & (k_row < r0 + NA_ROWS) & (k_col >= c0) & (k_col < c0 + NA_COLS)
    mask_add = jnp.where(valid, 0.0, NEG).astype(F32)

    outs = []
    for h in range(N_Q_HEADS):
        lo, hi = h * HEAD_DIM, (h + 1) * HEAD_DIM
        q = q_ref[0, :, lo:hi] * SCALE
        sc = jnp.concatenate(
            [lax.dot_general(q, k_ref[0, :, lo:hi], _NT, preferred_element_type=F32)
             for k_ref in (kp_ref, kc_ref, kn_ref)], axis=-1)
        sc = sc + bias_ref[h] + mask_add
        p = jnp.exp(sc - jnp.max(sc, axis=-1, keepdims=True))
        den = jnp.sum(p, axis=-1, keepdims=True)
        pb = p.astype(BF16)
        o = sum(jnp.dot(pb[:, c * tq:(c + 1) * tq], v_ref[0, :, lo:hi], preferred_element_type=F32)
                for c, v_ref in enumerate((vp_ref, vc_ref, vn_ref)))
        outs.append(o / den)
    for h2 in range(N_Q_HEADS // 2):
        pair = jnp.concatenate([outs[2 * h2], outs[2 * h2 + 1]], axis=-1)
        o_ref[0, :, 2 * h2 * HEAD_DIM:(2 * h2 + 2) * HEAD_DIM] = pair.astype(o_ref.dtype)

    _memory_attention(qm_ref[0] * SCALE, kvm_ref[0], o_ref)


def _nbr_attention(proj, kvm, bias):
    b, t, _ = proj.shape
    tq = NB_TQ
    n = t // tq
    n_rows = t // GRID_W
    assert n_rows >= NA_ROWS and t % tq == 0
    prev_map = lambda col: (lambda bi, i: (bi, jnp.maximum(i - 1, 0), col))
    cur_map = lambda col: (lambda bi, i: (bi, i, col))
    next_map = lambda col: (lambda bi, i: (bi, jnp.minimum(i + 1, n - 1), col))
    kv_spec = lambda m: pl.BlockSpec((1, tq, Q_WIDTH), m)
    kernel = functools.partial(_nbr_kernel, n_rows=n_rows)
    return pl.pallas_call(
        kernel,
        grid=(b, n),
        in_specs=[
            kv_spec(cur_map(0)),
            kv_spec(prev_map(1)), kv_spec(cur_map(1)), kv_spec(next_map(1)),
            kv_spec(prev_map(2)), kv_spec(cur_map(2)), kv_spec(next_map(2)),
            pl.BlockSpec((1, tq, MEM_WIDTH), cur_map(3 * Q_WIDTH // MEM_WIDTH)),
            pl.BlockSpec((1, MEM_LEN, 2 * MEM_WIDTH), lambda bi, i: (bi, 0, 0)),
            _resident((N_Q_HEADS, tq, 3 * tq), lambda bi, i: (0, 0, 0)),
        ],
        out_specs=pl.BlockSpec((1, tq, MIX_WIDTH), lambda bi, i: (bi, i, 0)),
        out_shape=jax.ShapeDtypeStruct((b, t, MIX_WIDTH), BF16),
        compiler_params=_params("parallel", "parallel"),
        name="nbr_attention",
    )(proj, proj, proj, proj, proj, proj, proj, proj, kvm, bias)


def _out_proj_kernel(x_ref, o_ref, w_ref, g_ref, xo_ref, h_ref):
    x = x_ref[...] + jnp.dot(o_ref[...], w_ref[...], preferred_element_type=F32)
    xo_ref[...] = x
    h_ref[...] = _rms(x, g_ref[...]).astype(h_ref.dtype)


def _out_proj(x, o, w, g, tm):
    n, d = x.shape
    return pl.pallas_call(
        _out_proj_kernel,
        grid=(n // tm,),
        in_specs=[
            pl.BlockSpec((tm, d), lambda i: (i, 0)),
            pl.BlockSpec((tm, o.shape[1]), lambda i: (i, 0)),
            _resident(w.shape, lambda i: (0, 0)),
            _resident((1, d), lambda i: (0, 0)),
        ],
        out_specs=[pl.BlockSpec((tm, d), lambda i: (i, 0)), pl.BlockSpec((tm, d), lambda i: (i, 0))],
        out_shape=[jax.ShapeDtypeStruct((n, d), F32), jax.ShapeDtypeStruct((n, d), BF16)],
        compiler_params=_params("parallel"),
        name="out_proj",
    )(x, o, w, g.reshape(1, d))


def _ffn_kernel(hp_ref, hc_ref, hn_ref, x_ref, wg_ref, wu_ref, cw_ref, cb_ref, wd_ref, gf_ref, o_ref, acc_ref,
                *, final_norm):
    i = pl.program_id(1)
    n = pl.num_programs(1)
    tm = hc_ref.shape[1]
    hc = hc_ref[0]
    hp = hp_ref[0]
    hn = hn_ref[0]
    has_prev = (i > 0).astype(F32)
    has_next = (i < n - 1).astype(F32)
    row = lax.broadcasted_iota(jnp.int32, (tm, 1), 0)
    acc_ref[...] = jnp.zeros_like(acc_ref)

    def chunk(c, carry):
        wg = wg_ref[c]
        g = jnp.dot(hc, wg, preferred_element_type=F32)
        g_before = jnp.dot(hp, wg, preferred_element_type=F32)[HALO - 1:HALO] * has_prev
        g_after = jnp.dot(hn, wg, preferred_element_type=F32)[0:1] * has_next
        g_prev = jnp.where(row == 0, g_before, pltpu.roll(g, 1, 0))
        g_next = jnp.where(row == tm - 1, g_after, pltpu.roll(g, tm - 1, 0))
        cw = cw_ref[c]
        gc = g_prev * cw[0:1] + g * cw[1:2] + g_next * cw[2:3] + cb_ref[c]
        u = jnp.dot(hc, wu_ref[c], preferred_element_type=F32)
        a = (gc * jax.nn.sigmoid(gc) * u).astype(BF16)
        acc_ref[...] += jnp.dot(a, wd_ref[c], preferred_element_type=F32)
        return carry

    lax.fori_loop(0, wg_ref.shape[0], chunk, 0)
    y = x_ref[0] + acc_ref[...]
    if final_norm:
        y = _rms(y, gf_ref[...])
    o_ref[0] = y


def _conv_ffn(x, h, w_gate, w_up, conv_w, conv_b, w_down, g_final, tm, final_norm):
    b, t, d = x.shape
    nc = w_gate.shape[0]
    per = tm // HALO
    n_halo = t // HALO
    kernel = functools.partial(_ffn_kernel, final_norm=final_norm)
    const3 = lambda bi, i: (0, 0, 0)
    return pl.pallas_call(
        kernel,
        grid=(b, t // tm),
        in_specs=[
            pl.BlockSpec((1, HALO, d), lambda bi, i: (bi, jnp.maximum(i * per - 1, 0), 0)),
            pl.BlockSpec((1, tm, d), lambda bi, i: (bi, i, 0)),
            pl.BlockSpec((1, HALO, d), lambda bi, i: (bi, jnp.minimum((i + 1) * per, n_halo - 1), 0)),
            pl.BlockSpec((1, tm, d), lambda bi, i: (bi, i, 0)),
            _resident(w_gate.shape, const3),
            _resident(w_up.shape, const3),
            _resident(conv_w.shape, const3),
            _resident(conv_b.shape, const3),
            _resident(w_down.shape, const3),
            _resident((1, d), lambda bi, i: (0, 0)),
        ],
        out_specs=pl.BlockSpec((1, tm, d), lambda bi, i: (bi, i, 0)),
        out_shape=jax.ShapeDtypeStruct((b, t, d), F32),
        scratch_shapes=[pltpu.VMEM((tm, d), F32)],
        compiler_params=_params("parallel", "parallel"),
        name="conv_ffn",
    )(h, h, h, x, w_gate, w_up, conv_w, conv_b, w_down, g_final.reshape(1, d))


def _chunk_cols(w):
    d = w.shape[0]
    return w.reshape(d, D_FF // FF_CHUNK, FF_CHUNK).transpose(1, 0, 2)


def _prepare_weights(g_mix, g_mem, w_in_a, sink_a, w_in_b, rpb_b, w_mem_kv, w_o, g_ffn, w_gate, w_up, conv_w,
                     conv_b, w_down, g_final):
    nc = D_FF // FF_CHUNK
    layers = []
    for i in range(DEPTH):
        j = i // 2
        layer = dict(
            g_mix=g_mix[i], g_mem=g_mem[i], g_ffn=g_ffn[i],
            w_mem_kv=w_mem_kv[i].astype(BF16), w_o=w_o[i].astype(BF16),
            w_gate=_chunk_cols(w_gate[i]).astype(BF16), w_up=_chunk_cols(w_up[i]).astype(BF16),
            conv_w=conv_w[i].reshape(CONV_W, nc, FF_CHUNK).transpose(1, 0, 2),
            conv_b=conv_b[i].reshape(nc, 1, FF_CHUNK),
            w_down=w_down[i].reshape(nc, FF_CHUNK, D_MODEL).astype(BF16),
        )
        if i % 2 == 0:
            layer.update(w_in=w_in_a[j].astype(BF16), sink=sink_a[j].astype(F32))
        else:
            layer.update(w_in=w_in_b[j].astype(BF16), bias=_nbr_bias_table(rpb_b[j]))
        layers.append(layer)
    return layers


def _trunk(x, mem, layers, g_final):
    b, t, d = x.shape
    n = b * t
    tm = 512
    x = x.reshape(n, d)
    mem2 = mem.reshape(b * MEM_LEN, d)
    for i, lw in enumerate(layers):
        proj = _norm_matmul(x, lw["g_mix"], lw["w_in"], tm).reshape(b, t, -1)
        kvm = _norm_matmul(mem2, lw["g_mem"], lw["w_mem_kv"], MEM_LEN).reshape(b, MEM_LEN, 2 * MEM_WIDTH)
        if i % 2 == 0:
            o = _window_attention(proj, kvm, lw["sink"], tq=512)
        else:
            o = _nbr_attention(proj, kvm, lw["bias"])
        x, h = _out_proj(x, o.reshape(n, MIX_WIDTH), lw["w_o"], lw["g_ffn"], tm)
        x = _conv_ffn(x.reshape(b, t, d), h.reshape(b, t, d), lw["w_gate"], lw["w_up"], lw["conv_w"],
                      lw["conv_b"], lw["w_down"], g_final, tm, final_norm=(i == DEPTH - 1)).reshape(n, d)
    return x.reshape(b, t, d)


def kernel(x_prompt, x_sample, mem_prompt, mem_sample, g_mix, g_mem, w_in_a, sink_a, w_in_b, rpb_b, w_mem_kv, w_o,
           g_ffn, w_gate, w_up, conv_w, conv_b, w_down, g_final):
    layers = _prepare_weights(g_mix, g_mem, w_in_a, sink_a, w_in_b, rpb_b, w_mem_kv, w_o, g_ffn, w_gate, w_up,
                              conv_w, conv_b, w_down, g_final)
    y_prompt = _trunk(x_prompt, mem_prompt, layers, g_final)
    y_sample = _trunk(x_sample, mem_sample, layers, g_final)
    return (y_prompt, y_sample)
```

```python
import functools

import numpy as np
import jax
import jax.numpy as jnp
from jax import lax
from jax.experimental import pallas as pl
from jax.experimental.pallas import tpu as pltpu

D_MODEL = 1024
DEPTH = 2
HEAD_DIM = 64
N_Q_HEADS = 12
N_KV_HEADS_A = 4
N_MEM_HEADS = 4
MEM_LEN = 256
WINDOW = 128
GRID_W = 64
NA_ROWS = 8
NA_COLS = 16
D_FF = 2816
CONV_W = 3
EPS = 1e-6
NEG = -1e30
Q_WIDTH = N_Q_HEADS * HEAD_DIM
KV_WIDTH_A = N_KV_HEADS_A * HEAD_DIM
MEM_WIDTH = N_MEM_HEADS * HEAD_DIM
MIX_WIDTH = Q_WIDTH + MEM_WIDTH
SCALE = HEAD_DIM ** -0.5

F32 = jnp.float32
BF16 = jnp.bfloat16

VMEM_LIMIT_BYTES = 56 * 1024 * 1024
FF_CHUNK = 256
HALO = 16

_NT = (((1,), (1,)), ((), ()))


def _params(*sem):
    return pltpu.CompilerParams(dimension_semantics=sem, vmem_limit_bytes=VMEM_LIMIT_BYTES)


def _resident(shape, index_map):
    return pl.BlockSpec(shape, index_map, pipeline_mode=pl.Buffered(1))


def _rms(x, g):
    return x * lax.rsqrt(jnp.mean(x * x, axis=-1, keepdims=True) + EPS) * g


def _norm_matmul_kernel(x_ref, g_ref, w_ref, o_ref):
    h = _rms(x_ref[...], g_ref[...]).astype(BF16)
    o_ref[...] = jnp.dot(h, w_ref[...], preferred_element_type=F32).astype(o_ref.dtype)


def _norm_matmul(x, g, w, tm):
    n, d = x.shape
    wdt = w.shape[1]
    return pl.pallas_call(
        _norm_matmul_kernel,
        grid=(n // tm,),
        in_specs=[
            pl.BlockSpec((tm, d), lambda i: (i, 0)),
            _resident((1, d), lambda i: (0, 0)),
            _resident((d, wdt), lambda i: (0, 0)),
        ],
        out_specs=pl.BlockSpec((tm, wdt), lambda i: (i, 0)),
        out_shape=jax.ShapeDtypeStruct((n, wdt), BF16),
        compiler_params=_params("parallel"),
        name="norm_matmul",
    )(x, g.reshape(1, d), w)


def _memory_attention(qm, kvm, o_ref):
    outs = []
    for h in range(N_MEM_HEADS):
        q = qm[:, h * HEAD_DIM:(h + 1) * HEAD_DIM]
        k = kvm[:, h * HEAD_DIM:(h + 1) * HEAD_DIM]
        v = kvm[:, MEM_WIDTH + h * HEAD_DIM:MEM_WIDTH + (h + 1) * HEAD_DIM]
        s = lax.dot_general(q, k, _NT, preferred_element_type=F32)
        p = jnp.exp(s - jnp.max(s, axis=-1, keepdims=True))
        den = jnp.sum(p, axis=-1, keepdims=True)
        outs.append(jnp.dot(p.astype(BF16), v, preferred_element_type=F32) / den)
    for j in range(N_MEM_HEADS // 2):
        pair = jnp.concatenate([outs[2 * j], outs[2 * j + 1]], axis=-1)
        o_ref[0, :, Q_WIDTH + 2 * j * HEAD_DIM:Q_WIDTH + (2 * j + 2) * HEAD_DIM] = pair.astype(o_ref.dtype)


def _alibi_slopes(n):
    return [float(v) for v in (2.0 ** (-8.0 * np.arange(1, n + 1, dtype=np.float32) / n)).astype(np.float32)]


def _window_kernel(sink_ref, q_ref, kp_ref, kc_ref, kn_ref, vp_ref, vc_ref, vn_ref, qm_ref, kvm_ref,
                   o_ref, kbuf, vbuf, *, tq, seq_len):
    i = pl.program_id(1)
    blk = WINDOW
    rep = N_Q_HEADS // N_KV_HEADS_A
    kbuf[0:blk] = kp_ref[0]
    kbuf[blk:blk + tq] = kc_ref[0]
    kbuf[blk + tq:] = kn_ref[0]
    vbuf[0:blk] = vp_ref[0]
    vbuf[blk:blk + tq] = vc_ref[0]
    vbuf[blk + tq:] = vn_ref[0]

    slopes = _alibi_slopes(N_Q_HEADS)
    kb_len = 3 * blk
    rows = rep * blk
    qq = lax.broadcasted_iota(jnp.int32, (rows, kb_len), 0) % blk
    kk = lax.broadcasted_iota(jnp.int32, (rows, kb_len), 1)
    rel = qq - kk + WINDOW
    arel = jnp.abs(rel)
    arel_f = arel.astype(F32)
    row_rep = lax.broadcasted_iota(jnp.int32, (rows, 1), 0) // blk

    for j in range(tq // blk):
        s_pos = i * tq + j * blk - WINDOW + kk
        valid = (arel <= WINDOW) & (s_pos >= 0) & (s_pos < seq_len)
        q_j = q_ref[0, j * blk:(j + 1) * blk, :] * SCALE
        outs = []
        for g in range(N_KV_HEADS_A):
            qg = jnp.concatenate(
                [q_j[:, (rep * g + r) * HEAD_DIM:(rep * g + r + 1) * HEAD_DIM] for r in range(rep)], axis=0)
            kg = kbuf[j * blk:j * blk + kb_len, g * HEAD_DIM:(g + 1) * HEAD_DIM]
            vg = vbuf[j * blk:j * blk + kb_len, g * HEAD_DIM:(g + 1) * HEAD_DIM]
            slope = jnp.full((rows, 1), slopes[rep * g], F32)
            sink = jnp.full((rows, 1), sink_ref[rep * g], F32)
            for r in range(1, rep):
                slope = jnp.where(row_rep == r, slopes[rep * g + r], slope)
                sink = jnp.where(row_rep == r, sink_ref[rep * g + r], sink)
            sc = lax.dot_general(qg, kg, _NT, preferred_element_type=F32) - slope * arel_f
            sc = jnp.where(valid, sc, NEG)
            m = jnp.maximum(jnp.max(sc, axis=-1, keepdims=True), sink)
            p = jnp.exp(sc - m)
            den = jnp.sum(p, axis=-1, keepdims=True) + jnp.exp(sink - m)
            o = jnp.dot(p.astype(BF16), vg, preferred_element_type=F32) / den
            outs.extend(o[r * blk:(r + 1) * blk] for r in range(rep))
        for h2 in range(N_Q_HEADS // 2):
            pair = jnp.concatenate([outs[2 * h2], outs[2 * h2 + 1]], axis=-1)
            o_ref[0, j * blk:(j + 1) * blk, 2 * h2 * HEAD_DIM:(2 * h2 + 2) * HEAD_DIM] = pair.astype(o_ref.dtype)

    _memory_attention(qm_ref[0] * SCALE, kvm_ref[0], o_ref)


def _window_attention(proj, kvm, sink, tq):
    b, t, _ = proj.shape
    blk = WINDOW
    n_blk = t // blk
    per = tq // blk
    kcol = Q_WIDTH // KV_WIDTH_A
    prev_map = lambda col: (lambda bi, i: (bi, jnp.maximum(i * per - 1, 0), col))
    cur_map = lambda col: (lambda bi, i: (bi, i, col))
    next_map = lambda col: (lambda bi, i: (bi, jnp.minimum((i + 1) * per, n_blk - 1), col))
    kernel = functools.partial(_window_kernel, tq=tq, seq_len=t)
    return pl.pallas_call(
        kernel,
        grid=(b, t // tq),
        in_specs=[
            pl.BlockSpec(memory_space=pltpu.SMEM),
            pl.BlockSpec((1, tq, Q_WIDTH), lambda bi, i: (bi, i, 0)),
            pl.BlockSpec((1, blk, KV_WIDTH_A), prev_map(kcol)),
            pl.BlockSpec((1, tq, KV_WIDTH_A), cur_map(kcol)),
            pl.BlockSpec((1, blk, KV_WIDTH_A), next_map(kcol)),
            pl.BlockSpec((1, blk, KV_WIDTH_A), prev_map(kcol + 1)),
            pl.BlockSpec((1, tq, KV_WIDTH_A), cur_map(kcol + 1)),
            pl.BlockSpec((1, blk, KV_WIDTH_A), next_map(kcol + 1)),
            pl.BlockSpec((1, tq, MEM_WIDTH), cur_map(kcol + 2)),
            pl.BlockSpec((1, MEM_LEN, 2 * MEM_WIDTH), lambda bi, i: (bi, 0, 0)),
        ],
        out_specs=pl.BlockSpec((1, tq, MIX_WIDTH), lambda bi, i: (bi, i, 0)),
        out_shape=jax.ShapeDtypeStruct((b, t, MIX_WIDTH), BF16),
        scratch_shapes=[pltpu.VMEM((tq + 2 * blk, KV_WIDTH_A), BF16),
                        pltpu.VMEM((tq + 2 * blk, KV_WIDTH_A), BF16)],
        compiler_params=_params("parallel", "parallel"),
        name="window_attention",
    )(sink, proj, proj, proj, proj, proj, proj, proj, proj, kvm)


NB_ROWS_PER_TILE = 4
NB_TQ = NB_ROWS_PER_TILE * GRID_W


def _nbr_bias_table(rpb):
    h, n_dr, n_dc = rpb.shape
    w = GRID_W
    lo = (w - 1) - (NA_COLS - 1)
    e = jnp.pad(rpb.astype(F32), ((0, 0), (0, 0), (lo, 2 * w - lo - n_dc)))
    toe = jnp.tile(e, (1, 1, w))[..., :w * (2 * w - 1)].reshape(h, n_dr, w, 2 * w - 1)
    blocks = toe[..., w - 1:]
    zero = jnp.zeros((h, w, w), F32)
    rows = []
    for a in range(NB_ROWS_PER_TILE):
        strip = []
        for b in range(3 * NB_ROWS_PER_TILE):
            dr = (b - NB_ROWS_PER_TILE) - a + (NA_ROWS - 1)
            strip.append(blocks[:, dr] if 0 <= dr < n_dr else zero)
        rows.append(jnp.concatenate(strip, axis=-1))
    return jnp.concatenate(rows, axis=1)


def _nbr_kernel(q_ref, kp_ref, kc_ref, kn_ref, vp_ref, vc_ref, vn_ref, qm_ref, kvm_ref, bias_ref,
                o_ref, *, n_rows):
    i = pl.program_id(1)
    tq = NB_TQ
    nk = 3 * tq
    qi = lax.broadcasted_iota(jnp.int32, (tq, nk), 0)
    ki = lax.broadcasted_iota(jnp.int32, (tq, nk), 1)
    q_row = i * NB_ROWS_PER_TILE + qi // GRID_W
    k_row = (i - 1) * NB_ROWS_PER_TILE + ki // GRID_W
    r0 = jnp.clip(q_row - NA_ROWS // 2, 0, n_rows - NA_ROWS)
    c0 = jnp.clip(qi % GRID_W - NA_COLS // 2, 0, GRID_W - NA_COLS)
    k_col = ki % GRID_W
    valid = (k_row >= r0) & (k_row < r0 + NA_ROWS) & (k_col >= c0) & (k_col < c0 + NA_COLS)
    mask_add = jnp.where(valid, 0.0, NEG).astype(F32)

    outs = []
    for h in range(N_Q_HEADS):
        lo, hi = h * HEAD_DIM, (h + 1) * HEAD_DIM
        q = q_ref[0, :, lo:hi] * SCALE
        sc = jnp.concatenate(
            [lax.dot_general(q, k_ref[0, :, lo:hi], _NT, preferred_element_type=F32)
             for k_ref in (kp_ref, kc_ref, kn_ref)], axis=-1)
        sc = sc + bias_ref[h] + mask_add
        p = jnp.exp(sc - jnp.max(sc, axis=-1, keepdims=True))
        den = jnp.sum(p, axis=-1, keepdims=True)
        pb = p.astype(BF16)
        o = sum(jnp.dot(pb[:, c * tq:(c + 1) * tq], v_ref[0, :, lo:hi], preferred_element_type=F32)
                for c, v_ref in enumerate((vp_ref, vc_ref, vn_ref)))
        outs.append(o / den)
    for h2 in range(N_Q_HEADS // 2):
        pair = jnp.concatenate([outs[2 * h2], outs[2 * h2 + 1]], axis=-1)
        o_ref[0, :, 2 * h2 * HEAD_DIM:(2 * h2 + 2) * HEAD_DIM] = pair.astype(o_ref.dtype)

    _memory_attention(qm_ref[0] * SCALE, kvm_ref[0], o_ref)


def _nbr_attention(proj, kvm, bias):
    b, t, _ = proj.shape
    tq = NB_TQ
    n = t // tq
    n_rows = t // GRID_W
    assert n_rows >= NA_ROWS and t % tq == 0
    prev_map = lambda col: (lambda bi, i: (bi, jnp.maximum(i - 1, 0), col))
    cur_map = lambda col: (lambda bi, i: (bi, i, col))
    next_map = lambda col: (lambda bi, i: (bi, jnp.minimum(i + 1, n - 1), col))
    kv_spec = lambda m: pl.BlockSpec((1, tq, Q_WIDTH), m)
    kernel = functools.partial(_nbr_kernel, n_rows=n_rows)
    return pl.pallas_call(
        kernel,
        grid=(b, n),
        in_specs=[
            kv_spec(cur_map(0)),
            kv_spec(prev_map(1)), kv_spec(cur_map(1)), kv_spec(next_map(1)),
            kv_spec(prev_map(2)), kv_spec(cur_map(2)), kv_spec(next_map(2)),
            pl.BlockSpec((1, tq, MEM_WIDTH), cur_map(3 * Q_WIDTH // MEM_WIDTH)),
            pl.BlockSpec((1, MEM_LEN, 2 * MEM_WIDTH), lambda bi, i: (bi, 0, 0)),
            _resident((N_Q_HEADS, tq, 3 * tq), lambda bi, i: (0, 0, 0)),
        ],
        out_specs=pl.BlockSpec((1, tq, MIX_WIDTH), lambda bi, i: (bi, i, 0)),
        out_shape=jax.ShapeDtypeStruct((b, t, MIX_WIDTH), BF16),
        compiler_params=_params("parallel", "parallel"),
        name="nbr_attention",
    )(proj, proj, proj, proj, proj, proj, proj, proj, kvm, bias)


def _out_proj_kernel(x_ref, o_ref, w_ref, g_ref, xo_ref, h_ref):
    x = x_ref[...] + jnp.dot(o_ref[...], w_ref[...], preferred_element_type=F32)
    xo_ref[...] = x
    h_ref[...] = _rms(x, g_ref[...]).astype(h_ref.dtype)


def _out_proj(x, o, w, g, tm):
    n, d = x.shape
    return pl.pallas_call(
        _out_proj_kernel,
        grid=(n // tm,),
        in_specs=[
            pl.BlockSpec((tm, d), lambda i: (i, 0)),
            pl.BlockSpec((tm, o.shape[1]), lambda i: (i, 0)),
            _resident(w.shape, lambda i: (0, 0)),
            _resident((1, d), lambda i: (0, 0)),
        ],
        out_specs=[pl.BlockSpec((tm, d), lambda i: (i, 0)), pl.BlockSpec((tm, d), lambda i: (i, 0))],
        out_shape=[jax.ShapeDtypeStruct((n, d), F32), jax.ShapeDtypeStruct((n, d), BF16)],
        compiler_params=_params("parallel"),
        name="out_proj",
    )(x, o, w, g.reshape(1, d))


def _ffn_kernel(hp_ref, hc_ref, hn_ref, x_ref, wg_ref, wu_ref, cw_ref, cb_ref, wd_ref, gf_ref, o_ref,
                hext_ref, act_ref, *, final_norm):
    i = pl.program_id(1)
    n = pl.num_programs(1)
    tm = hc_ref.shape[1]
    ext = tm + 2 * HALO
    hext_ref[0:HALO] = jnp.where(i > 0, hp_ref[0], jnp.zeros_like(hp_ref[0]))
    hext_ref[HALO:HALO + tm] = hc_ref[0]
    hext_ref[HALO + tm:] = jnp.where(i < n - 1, hn_ref[0], jnp.zeros_like(hn_ref[0]))

    for c in range(wg_ref.shape[0]):
        g_ext = jnp.dot(hext_ref[...], wg_ref[c], preferred_element_type=F32)
        g = g_ext[HALO:HALO + tm]
        g_prev = pltpu.roll(g_ext, 1, 0)[HALO:HALO + tm]
        g_next = pltpu.roll(g_ext, ext - 1, 0)[HALO:HALO + tm]
        cw = cw_ref[c]
        gc = g_prev * cw[0:1] + g * cw[1:2] + g_next * cw[2:3] + cb_ref[c]
        u = jnp.dot(hext_ref[HALO:HALO + tm], wu_ref[c], preferred_element_type=F32)
        act_ref[:, c * FF_CHUNK:(c + 1) * FF_CHUNK] = (gc * jax.nn.sigmoid(gc) * u).astype(BF16)

    y = x_ref[0] + jnp.dot(act_ref[...], wd_ref[...], preferred_element_type=F32)
    if final_norm:
        y = _rms(y, gf_ref[...])
    o_ref[0] = y


def _conv_ffn(x, h, w_gate, w_up, conv_w, conv_b, w_down, g_final, tm, final_norm):
    b, t, d = x.shape
    nc = w_gate.shape[0]
    per = tm // HALO
    n_halo = t // HALO
    kernel = functools.partial(_ffn_kernel, final_norm=final_norm)
    const3 = lambda bi, i: (0, 0, 0)
    const2 = lambda bi, i: (0, 0)
    return pl.pallas_call(
        kernel,
        grid=(b, t // tm),
        in_specs=[
            pl.BlockSpec((1, HALO, d), lambda bi, i: (bi, jnp.maximum(i * per - 1, 0), 0)),
            pl.BlockSpec((1, tm, d), lambda bi, i: (bi, i, 0)),
            pl.BlockSpec((1, HALO, d), lambda bi, i: (bi, jnp.minimum((i + 1) * per, n_halo - 1), 0)),
            pl.BlockSpec((1, tm, d), lambda bi, i: (bi, i, 0)),
            _resident(w_gate.shape, const3),
            _resident(w_up.shape, const3),
            _resident(conv_w.shape, const3),
            _resident(conv_b.shape, const3),
            _resident(w_down.shape, const2),
            _resident((1, d), const2),
        ],
        out_specs=pl.BlockSpec((1, tm, d), lambda bi, i: (bi, i, 0)),
        out_shape=jax.ShapeDtypeStruct((b, t, d), F32),
        scratch_shapes=[pltpu.VMEM((tm + 2 * HALO, d), BF16), pltpu.VMEM((tm, D_FF), BF16)],
        compiler_params=_params("parallel", "parallel"),
        name="conv_ffn",
    )(h, h, h, x, w_gate, w_up, conv_w, conv_b, w_down, g_final.reshape(1, d))


def _chunk_cols(w):
    d = w.shape[0]
    return w.reshape(d, D_FF // FF_CHUNK, FF_CHUNK).transpose(1, 0, 2)


def _prepare_weights(g_mix, g_mem, w_in_a, sink_a, w_in_b, rpb_b, w_mem_kv, w_o, g_ffn, w_gate, w_up, conv_w,
                     conv_b, w_down, g_final):
    nc = D_FF // FF_CHUNK
    layers = []
    for i in range(DEPTH):
        j = i // 2
        layer = dict(
            g_mix=g_mix[i], g_mem=g_mem[i], g_ffn=g_ffn[i],
            w_mem_kv=w_mem_kv[i].astype(BF16), w_o=w_o[i].astype(BF16),
            w_gate=_chunk_cols(w_gate[i]).astype(BF16), w_up=_chunk_cols(w_up[i]).astype(BF16),
            conv_w=conv_w[i].reshape(CONV_W, nc, FF_CHUNK).transpose(1, 0, 2),
            conv_b=conv_b[i].reshape(nc, 1, FF_CHUNK),
            w_down=w_down[i].astype(BF16),
        )
        if i % 2 == 0:
            layer.update(w_in=w_in_a[j].astype(BF16), sink=sink_a[j].astype(F32))
        else:
            layer.update(w_in=w_in_b[j].astype(BF16), bias=_nbr_bias_table(rpb_b[j]))
        layers.append(layer)
    return layers


def _trunk(x, mem, layers, g_final):
    b, t, d = x.shape
    n = b * t
    tm = 512
    x = x.reshape(n, d)
    mem2 = mem.reshape(b * MEM_LEN, d)
    for i, lw in enumerate(layers):
        proj = _norm_matmul(x, lw["g_mix"], lw["w_in"], tm).reshape(b, t, -1)
        kvm = _norm_matmul(mem2, lw["g_mem"], lw["w_mem_kv"], MEM_LEN).reshape(b, MEM_LEN, 2 * MEM_WIDTH)
        if i % 2 == 0:
            o = _window_attention(proj, kvm, lw["sink"], tq=512)
        else:
            o = _nbr_attention(proj, kvm, lw["bias"])
        x, h = _out_proj(x, o.reshape(n, MIX_WIDTH), lw["w_o"], lw["g_ffn"], tm)
        x = _conv_ffn(x.reshape(b, t, d), h.reshape(b, t, d), lw["w_gate"], lw["w_up"], lw["conv_w"],
                      lw["conv_b"], lw["w_down"], g_final, tm, final_norm=(i == DEPTH - 1)).reshape(n, d)
    return x.reshape(b, t, d)


def kernel(x_prompt, x_sample, mem_prompt, mem_sample, g_mix, g_mem, w_in_a, sink_a, w_in_b, rpb_b, w_mem_kv, w_o,
           g_ffn, w_gate, w_up, conv_w, conv_b, w_down, g_final):
    layers = _prepare_weights(g_mix, g_mem, w_in_a, sink_a, w_in_b, rpb_b, w_mem_kv, w_o, g_ffn, w_gate, w_up,
                              conv_w, conv_b, w_down, g_final)
    y_prompt = _trunk(x_prompt, mem_prompt, layers, g_final)
    y_sample = _trunk(x_sample, mem_sample, layers, g_final)
    return (y_prompt, y_sample)
```

```python
import functools

import numpy as np
import jax
import jax.numpy as jnp
from jax import lax
from jax.experimental import pallas as pl
from jax.experimental.pallas import tpu as pltpu

D_MODEL = 1024
DEPTH = 2
HEAD_DIM = 64
N_Q_HEADS = 12
N_KV_HEADS_A = 4
N_MEM_HEADS = 4
MEM_LEN = 256
WINDOW = 128
GRID_W = 64
NA_ROWS = 8
NA_COLS = 16
D_FF = 2816
CONV_W = 3
EPS = 1e-6
NEG = -1e30
Q_WIDTH = N_Q_HEADS * HEAD_DIM
KV_WIDTH_A = N_KV_HEADS_A * HEAD_DIM
MEM_WIDTH = N_MEM_HEADS * HEAD_DIM
MIX_WIDTH = Q_WIDTH + MEM_WIDTH
SCALE = HEAD_DIM ** -0.5
REP_A = N_Q_HEADS // N_KV_HEADS_A

F32 = jnp.float32
BF16 = jnp.bfloat16

LANES = 128
VMEM_LIMIT_BYTES = 56 * 1024 * 1024
FF_CHUNK = 256
HALO = 16

_NT = (((1,), (1,)), ((), ()))


def _params(*sem):
    return pltpu.CompilerParams(dimension_semantics=sem, vmem_limit_bytes=VMEM_LIMIT_BYTES)


def _resident(shape, index_map):
    return pl.BlockSpec(shape, index_map, pipeline_mode=pl.Buffered(1))


def _rms(x, g):
    return x * lax.rsqrt(jnp.mean(x * x, axis=-1, keepdims=True) + EPS) * g


def _proj_kernel(x_ref, g_ref, w_ref, wt_ref, tok_ref, feat_ref):
    h = _rms(x_ref[...], g_ref[...]).astype(BF16)
    tok_ref[...] = jnp.dot(h, w_ref[...], preferred_element_type=F32).astype(tok_ref.dtype)
    feat_ref[0] = lax.dot_general(wt_ref[...], h, _NT, preferred_element_type=F32).astype(feat_ref.dtype)


def _norm_project(x, g, w_tok, w_feat_t, batch, tm):
    n, d = x.shape
    t = n // batch
    per_seq = t // tm
    wdt = w_tok.shape[1]
    wft = w_feat_t.shape[0]
    return pl.pallas_call(
        _proj_kernel,
        grid=(n // tm,),
        in_specs=[
            pl.BlockSpec((tm, d), lambda i: (i, 0)),
            _resident((1, d), lambda i: (0, 0)),
            _resident((d, wdt), lambda i: (0, 0)),
            _resident((wft, d), lambda i: (0, 0)),
        ],
        out_specs=[pl.BlockSpec((tm, wdt), lambda i: (i, 0)),
                   pl.BlockSpec((1, wft, tm), lambda i: (i // per_seq, 0, i % per_seq))],
        out_shape=[jax.ShapeDtypeStruct((n, wdt), BF16), jax.ShapeDtypeStruct((batch, wft, t), BF16)],
        compiler_params=_params("parallel"),
        name="norm_project",
    )(x, g.reshape(1, d), w_tok, w_feat_t)


def _low_lanes(rows):
    return lax.broadcasted_iota(jnp.int32, (rows, LANES), 1) < HEAD_DIM


def _split_heads(q_tile, low):
    zero = jnp.zeros_like(q_tile)
    return jnp.where(low, q_tile, zero), jnp.where(low, zero, q_tile)


def _with_ones(v):
    return jnp.concatenate([v, jnp.ones_like(v)], axis=1)


def _memory_attention(qm, kmt_ref, vm_ref, o_ref):
    tq = qm.shape[0]
    low = _low_lanes(tq)
    for p in range(MEM_WIDTH // LANES):
        cols = slice(p * LANES, (p + 1) * LANES)
        qs = jnp.concatenate(_split_heads(qm[:, cols], low), axis=0)
        s = jnp.dot(qs, kmt_ref[0, cols, :], preferred_element_type=F32)
        pr = jnp.exp(s - jnp.max(s, axis=-1, keepdims=True)).astype(BF16)
        oa = jnp.dot(pr, _with_ones(vm_ref[0, :, cols]), preferred_element_type=F32)
        o = oa[:, :LANES] / oa[:, LANES:]
        o_ref[0, :, Q_WIDTH + p * LANES:Q_WIDTH + (p + 1) * LANES] = (
            jnp.where(low, o[:tq], o[tq:]).astype(o_ref.dtype))


def _window_head(pair, row_block):
    return REP_A * (2 * pair + row_block % 2) + row_block // 2


def _window_head_order():
    return [_window_head(p, rb) for p in range(N_KV_HEADS_A // 2) for rb in range(2 * REP_A)]


def _window_bias_table():
    slopes = (2.0 ** (-8.0 * np.arange(1, N_Q_HEADS + 1, dtype=np.float32) / N_Q_HEADS)).astype(np.float32)
    blk = WINDOW
    qq = np.arange(blk)[:, None]
    kk = np.arange(3 * blk)[None, :]
    arel = np.abs(qq - kk + WINDOW)
    in_window = arel <= WINDOW
    in_seq = [kk >= blk, np.ones_like(kk, bool), kk < 2 * blk]
    out = np.empty((3, N_KV_HEADS_A // 2, 2 * REP_A, blk, 3 * blk), np.float32)
    for var in range(3):
        for p in range(N_KV_HEADS_A // 2):
            for rb in range(2 * REP_A):
                alibi = -slopes[_window_head(p, rb)] * arel.astype(np.float32)
                out[var, p, rb] = np.where(in_window & in_seq[var], alibi, np.float32(NEG))
    return out.reshape(3 * (N_KV_HEADS_A // 2), 2 * REP_A * blk, 3 * blk)


def _window_kernel(sink_ref, q_ref, ktp_ref, ktc_ref, ktn_ref, vp_ref, vc_ref, vn_ref, qm_ref, kmt_ref, vm_ref,
                   bias_ref, o_ref, ktbuf, vbuf, *, tq, n_blk):
    i = pl.program_id(1)
    blk = WINDOW
    n_pair = N_KV_HEADS_A // 2
    ktbuf[:, 0:blk] = ktp_ref[0]
    ktbuf[:, blk:blk + tq] = ktc_ref[0]
    ktbuf[:, blk + tq:] = ktn_ref[0]
    vbuf[0:blk] = vp_ref[0]
    vbuf[blk:blk + tq] = vc_ref[0]
    vbuf[blk + tq:] = vn_ref[0]

    low = _low_lanes(blk)
    for j in range(tq // blk):
        g_blk = i * (tq // blk) + j
        variant = jnp.where(g_blk == 0, 0, jnp.where(g_blk == n_blk - 1, 2, 1))
        rows = slice(j * blk, (j + 1) * blk)
        keys = slice(j * blk, j * blk + 3 * blk)
        for p in range(n_pair):
            tiles = []
            sinks = []
            for t in range(REP_A):
                c0 = (REP_A * p + t) * LANES
                tiles.extend(_split_heads(q_ref[0, rows, c0:c0 + LANES] * SCALE, low))
                sinks.extend(jnp.full((blk, 1), sink_ref[_window_head(p, 2 * t + hi)], F32) for hi in range(2))
            qs = jnp.concatenate(tiles, axis=0)
            sink = jnp.concatenate(sinks, axis=0)
            sc = jnp.dot(qs, ktbuf[p * LANES:(p + 1) * LANES, keys], preferred_element_type=F32)
            sc = sc + bias_ref[variant * n_pair + p]
            m = jnp.maximum(jnp.max(sc, axis=-1, keepdims=True), sink)
            pr = jnp.exp(sc - m).astype(BF16)
            oa = jnp.dot(pr, _with_ones(vbuf[keys, p * LANES:(p + 1) * LANES]), preferred_element_type=F32)
            o = oa[:, :LANES] / (oa[:, LANES:] + jnp.exp(sink - m))
            for t in range(REP_A):
                c0 = (REP_A * p + t) * LANES
                o_ref[0, rows, c0:c0 + LANES] = jnp.where(
                    low, o[2 * t * blk:(2 * t + 1) * blk], o[(2 * t + 1) * blk:(2 * t + 2) * blk]).astype(o_ref.dtype)

    _memory_attention(qm_ref[0] * SCALE, kmt_ref, vm_ref, o_ref)


def _window_attention(tok, kt, kmt, vm, sink, bias, tq):
    b, t, _ = tok.shape
    blk = WINDOW
    n_blk = t // blk
    per = tq // blk
    assert n_blk >= 2 and t % tq == 0
    vcol = Q_WIDTH // KV_WIDTH_A
    prev_i = lambda i: jnp.maximum(i * per - 1, 0)
    next_i = lambda i: jnp.minimum((i + 1) * per, n_blk - 1)
    kernel = functools.partial(_window_kernel, tq=tq, n_blk=n_blk)
    return pl.pallas_call(
        kernel,
        grid=(b, t // tq),
        in_specs=[
            pl.BlockSpec(memory_space=pltpu.SMEM),
            pl.BlockSpec((1, tq, Q_WIDTH), lambda bi, i: (bi, i, 0)),
            pl.BlockSpec((1, KV_WIDTH_A, blk), lambda bi, i: (bi, 0, prev_i(i))),
            pl.BlockSpec((1, KV_WIDTH_A, tq), lambda bi, i: (bi, 0, i)),
            pl.BlockSpec((1, KV_WIDTH_A, blk), lambda bi, i: (bi, 0, next_i(i))),
            pl.BlockSpec((1, blk, KV_WIDTH_A), lambda bi, i: (bi, prev_i(i), vcol)),
            pl.BlockSpec((1, tq, KV_WIDTH_A), lambda bi, i: (bi, i, vcol)),
            pl.BlockSpec((1, blk, KV_WIDTH_A), lambda bi, i: (bi, next_i(i), vcol)),
            pl.BlockSpec((1, tq, MEM_WIDTH), lambda bi, i: (bi, i, vcol + 1)),
            pl.BlockSpec((1, MEM_WIDTH, MEM_LEN), lambda bi, i: (bi, 0, 0)),
            pl.BlockSpec((1, MEM_LEN, MEM_WIDTH), lambda bi, i: (bi, 0, 0)),
            _resident(bias.shape, lambda bi, i: (0, 0, 0)),
        ],
        out_specs=pl.BlockSpec((1, tq, MIX_WIDTH), lambda bi, i: (bi, i, 0)),
        out_shape=jax.ShapeDtypeStruct((b, t, MIX_WIDTH), BF16),
        scratch_shapes=[pltpu.VMEM((KV_WIDTH_A, tq + 2 * blk), BF16),
                        pltpu.VMEM((tq + 2 * blk, KV_WIDTH_A), BF16)],
        compiler_params=_params("parallel", "parallel"),
        name="window_attention",
    )(sink, tok, kt, kt, kt, tok, tok, tok, tok, kmt, vm, bias)


NB_ROWS_PER_TILE = 4
NB_TQ = NB_ROWS_PER_TILE * GRID_W


def _nbr_bias_table(rpb):
    h, n_dr, n_dc = rpb.shape
    w = GRID_W
    lo = (w - 1) - (NA_COLS - 1)
    e = jnp.pad(rpb.astype(F32), ((0, 0), (0, 0), (lo, 2 * w - lo - n_dc)))
    toe = jnp.tile(e, (1, 1, w))[..., :w * (2 * w - 1)].reshape(h, n_dr, w, 2 * w - 1)
    blocks = toe[..., w - 1:]
    zero = jnp.zeros((h, w, w), F32)
    rows = []
    for a in range(NB_ROWS_PER_TILE):
        strip = []
        for b in range(3 * NB_ROWS_PER_TILE):
            dr = (b - NB_ROWS_PER_TILE) - a + (NA_ROWS - 1)
            strip.append(blocks[:, dr] if 0 <= dr < n_dr else zero)
        rows.append(jnp.concatenate(strip, axis=-1))
    return jnp.concatenate(rows, axis=1)


def _nbr_mask_table(n_tiles, n_rows):
    qi = np.arange(NB_TQ)[:, None]
    ki = np.arange(3 * NB_TQ)[None, :]
    c0 = np.clip(qi % GRID_W - NA_COLS // 2, 0, GRID_W - NA_COLS)
    k_col = ki % GRID_W
    col_ok = (k_col >= c0) & (k_col < c0 + NA_COLS)
    out = []
    for i in (0, min(1, n_tiles - 1), n_tiles - 1):
        q_row = i * NB_ROWS_PER_TILE + qi // GRID_W
        k_row = (i - 1) * NB_ROWS_PER_TILE + ki // GRID_W
        r0 = np.clip(q_row - NA_ROWS // 2, 0, n_rows - NA_ROWS)
        row_ok = (k_row >= r0) & (k_row < r0 + NA_ROWS)
        out.append(np.where(row_ok & col_ok, np.float32(0.0), np.float32(NEG)))
    return np.stack(out).astype(np.float32)


def _nbr_kernel(q_ref, ktp_ref, ktc_ref, ktn_ref, vp_ref, vc_ref, vn_ref, qm_ref, kmt_ref, vm_ref, bias_ref,
                mask_ref, o_ref):
    tq = NB_TQ
    low = _low_lanes(tq)
    mask = mask_ref[0]
    for p in range(Q_WIDTH // LANES):
        cols = slice(p * LANES, (p + 1) * LANES)
        qs = jnp.concatenate(_split_heads(q_ref[0, :, cols] * SCALE, low), axis=0)
        kt = jnp.concatenate([ktp_ref[0, cols, :], ktc_ref[0, cols, :], ktn_ref[0, cols, :]], axis=1)
        sc = jnp.dot(qs, kt, preferred_element_type=F32)
        sc = sc + jnp.concatenate([bias_ref[2 * p] + mask, bias_ref[2 * p + 1] + mask], axis=0)
        pr = jnp.exp(sc - jnp.max(sc, axis=-1, keepdims=True)).astype(BF16)
        v = jnp.concatenate([vp_ref[0, :, cols], vc_ref[0, :, cols], vn_ref[0, :, cols]], axis=0)
        oa = jnp.dot(pr, _with_ones(v), preferred_element_type=F32)
        o = oa[:, :LANES] / oa[:, LANES:]
        o_ref[0, :, cols] = jnp.where(low, o[:tq], o[tq:]).astype(o_ref.dtype)

    _memory_attention(qm_ref[0] * SCALE, kmt_ref, vm_ref, o_ref)


def _nbr_attention(tok, kt, kmt, vm, bias):
    b, t, _ = tok.shape
    tq = NB_TQ
    n = t // tq
    n_rows = t // GRID_W
    assert n_rows >= NA_ROWS and t % tq == 0 and n >= 2
    mask = jnp.asarray(_nbr_mask_table(n, n_rows))
    prev_i = lambda i: jnp.maximum(i - 1, 0)
    next_i = lambda i: jnp.minimum(i + 1, n - 1)
    kt_spec = lambda f: pl.BlockSpec((1, Q_WIDTH, tq), lambda bi, i: (bi, 0, f(i)))
    v_spec = lambda f: pl.BlockSpec((1, tq, Q_WIDTH), lambda bi, i: (bi, f(i), 1))
    same = lambda i: i
    return pl.pallas_call(
        _nbr_kernel,
        grid=(b, n),
        in_specs=[
            pl.BlockSpec((1, tq, Q_WIDTH), lambda bi, i: (bi, i, 0)),
            kt_spec(prev_i), kt_spec(same), kt_spec(next_i),
            v_spec(prev_i), v_spec(same), v_spec(next_i),
            pl.BlockSpec((1, tq, MEM_WIDTH), lambda bi, i: (bi, i, 2 * Q_WIDTH // MEM_WIDTH)),
            pl.BlockSpec((1, MEM_WIDTH, MEM_LEN), lambda bi, i: (bi, 0, 0)),
            pl.BlockSpec((1, MEM_LEN, MEM_WIDTH), lambda bi, i: (bi, 0, 0)),
            _resident((N_Q_HEADS, tq, 3 * tq), lambda bi, i: (0, 0, 0)),
            pl.BlockSpec((1, tq, 3 * tq), lambda bi, i: (jnp.where(i == 0, 0, jnp.where(i == n - 1, 2, 1)), 0, 0)),
        ],
        out_specs=pl.BlockSpec((1, tq, MIX_WIDTH), lambda bi, i: (bi, i, 0)),
        out_shape=jax.ShapeDtypeStruct((b, t, MIX_WIDTH), BF16),
        compiler_params=_params("parallel", "parallel"),
        name="nbr_attention",
    )(tok, kt, kt, kt, tok, tok, tok, tok, kmt, vm, bias, mask)


def _out_proj_kernel(x_ref, o_ref, w_ref, g_ref, xo_ref, h_ref):
    x = x_ref[...] + jnp.dot(o_ref[...], w_ref[...], preferred_element_type=F32)
    xo_ref[...] = x
    h_ref[...] = _rms(x, g_ref[...]).astype(h_ref.dtype)


def _out_proj(x, o, w, g, tm):
    n, d = x.shape
    return pl.pallas_call(
        _out_proj_kernel,
        grid=(n // tm,),
        in_specs=[
            pl.BlockSpec((tm, d), lambda i: (i, 0)),
            pl.BlockSpec((tm, o.shape[1]), lambda i: (i, 0)),
            _resident(w.shape, lambda i: (0, 0)),
            _resident((1, d), lambda i: (0, 0)),
        ],
        out_specs=[pl.BlockSpec((tm, d), lambda i: (i, 0)), pl.BlockSpec((tm, d), lambda i: (i, 0))],
        out_shape=[jax.ShapeDtypeStruct((n, d), F32), jax.ShapeDtypeStruct((n, d), BF16)],
        compiler_params=_params("parallel"),
        name="out_proj",
    )(x, o, w, g.reshape(1, d))


def _ffn_kernel(hp_ref, hc_ref, hn_ref, x_ref, wg_ref, wu_ref, cw_ref, cb_ref, wd_ref, gf_ref, o_ref,
                hext_ref, act_ref, *, final_norm):
    i = pl.program_id(1)
    n = pl.num_programs(1)
    tm = hc_ref.shape[1]
    ext = tm + 2 * HALO
    hext_ref[0:HALO] = jnp.where(i > 0, hp_ref[0], jnp.zeros_like(hp_ref[0]))
    hext_ref[HALO:HALO + tm] = hc_ref[0]
    hext_ref[HALO + tm:] = jnp.where(i < n - 1, hn_ref[0], jnp.zeros_like(hn_ref[0]))

    for c in range(wg_ref.shape[0]):
        g_ext = jnp.dot(hext_ref[...], wg_ref[c], preferred_element_type=F32)
        g = g_ext[HALO:HALO + tm]
        g_prev = pltpu.roll(g_ext, 1, 0)[HALO:HALO + tm]
        g_next = pltpu.roll(g_ext, ext - 1, 0)[HALO:HALO + tm]
        cw = cw_ref[c]
        gc = g_prev * cw[0:1] + g * cw[1:2] + g_next * cw[2:3] + cb_ref[c]
        u = jnp.dot(hext_ref[HALO:HALO + tm], wu_ref[c], preferred_element_type=F32)
        act_ref[:, c * FF_CHUNK:(c + 1) * FF_CHUNK] = (gc * jax.nn.sigmoid(gc) * u).astype(BF16)

    y = x_ref[0] + jnp.dot(act_ref[...], wd_ref[...], preferred_element_type=F32)
    if final_norm:
        y = _rms(y, gf_ref[...])
    o_ref[0] = y


def _conv_ffn(x, h, w_gate, w_up, conv_w, conv_b, w_down, g_final, tm, final_norm):
    b, t, d = x.shape
    per = tm // HALO
    n_halo = t // HALO
    kernel = functools.partial(_ffn_kernel, final_norm=final_norm)
    const3 = lambda bi, i: (0, 0, 0)
    const2 = lambda bi, i: (0, 0)
    return pl.pallas_call(
        kernel,
        grid=(b, t // tm),
        in_specs=[
            pl.BlockSpec((1, HALO, d), lambda bi, i: (bi, jnp.maximum(i * per - 1, 0), 0)),
            pl.BlockSpec((1, tm, d), lambda bi, i: (bi, i, 0)),
            pl.BlockSpec((1, HALO, d), lambda bi, i: (bi, jnp.minimum((i + 1) * per, n_halo - 1), 0)),
            pl.BlockSpec((1, tm, d), lambda bi, i: (bi, i, 0)),
            _resident(w_gate.shape, const3),
            _resident(w_up.shape, const3),
            _resident(conv_w.shape, const3),
            _resident(conv_b.shape, const3),
            _resident(w_down.shape, const2),
            _resident((1, d), const2),
        ],
        out_specs=pl.BlockSpec((1, tm, d), lambda bi, i: (bi, i, 0)),
        out_shape=jax.ShapeDtypeStruct((b, t, d), F32),
        scratch_shapes=[pltpu.VMEM((tm + 2 * HALO, d), BF16), pltpu.VMEM((tm, D_FF), BF16)],
        compiler_params=_params("parallel", "parallel"),
        name="conv_ffn",
    )(h, h, h, x, w_gate, w_up, conv_w, conv_b, w_down, g_final.reshape(1, d))


def _chunk_cols(w):
    d = w.shape[0]
    return w.reshape(d, D_FF // FF_CHUNK, FF_CHUNK).transpose(1, 0, 2)


def _prepare_weights(g_mix, g_mem, w_in_a, sink_a, w_in_b, rpb_b, w_mem_kv, w_o, g_ffn, w_gate, w_up, conv_w,
                     conv_b, w_down, g_final):
    nc = D_FF // FF_CHUNK
    head_cols = np.arange(Q_WIDTH).reshape(N_Q_HEADS, HEAD_DIM)
    perm_a = head_cols[_window_head_order()].reshape(-1)
    layers = []
    for i in range(DEPTH):
        j = i // 2
        wo = w_o[i]
        if i % 2 == 0:
            w = w_in_a[j]
            wq, wk, wv, wqm = (w[:, :Q_WIDTH], w[:, Q_WIDTH:Q_WIDTH + KV_WIDTH_A],
                               w[:, Q_WIDTH + KV_WIDTH_A:Q_WIDTH + 2 * KV_WIDTH_A], w[:, Q_WIDTH + 2 * KV_WIDTH_A:])
            wq = wq[:, perm_a]
            wo = jnp.concatenate([wo[:Q_WIDTH][perm_a], wo[Q_WIDTH:]], axis=0)
            extra = dict(sink=sink_a[j].astype(F32), bias=jnp.asarray(_window_bias_table()))
        else:
            w = w_in_b[j]
            wq, wk, wv, wqm = (w[:, :Q_WIDTH], w[:, Q_WIDTH:2 * Q_WIDTH], w[:, 2 * Q_WIDTH:3 * Q_WIDTH],
                               w[:, 3 * Q_WIDTH:])
            extra = dict(bias=_nbr_bias_table(rpb_b[j]))
        layers.append(dict(
            g_mix=g_mix[i], g_mem=g_mem[i], g_ffn=g_ffn[i],
            w_tok=jnp.concatenate([wq, wv, wqm], axis=1).astype(BF16), w_kt=wk.T.astype(BF16),
            w_vm=w_mem_kv[i][:, MEM_WIDTH:].astype(BF16), w_kmt=w_mem_kv[i][:, :MEM_WIDTH].T.astype(BF16),
            w_o=wo.astype(BF16),
            w_gate=_chunk_cols(w_gate[i]).astype(BF16), w_up=_chunk_cols(w_up[i]).astype(BF16),
            conv_w=conv_w[i].reshape(CONV_W, nc, FF_CHUNK).transpose(1, 0, 2),
            conv_b=conv_b[i].reshape(nc, 1, FF_CHUNK),
            w_down=w_down[i].astype(BF16), **extra))
    return layers


def _trunk(x, mem, layers, g_final):
    b, t, d = x.shape
    n = b * t
    tm = 512
    x = x.reshape(n, d)
    mem2 = mem.reshape(b * MEM_LEN, d)
    for i, lw in enumerate(layers):
        tok, kt = _norm_project(x, lw["g_mix"], lw["w_tok"], lw["w_kt"], b, tm)
        vm, kmt = _norm_project(mem2, lw["g_mem"], lw["w_vm"], lw["w_kmt"], b, MEM_LEN)
        tok = tok.reshape(b, t, -1)
        vm = vm.reshape(b, MEM_LEN, MEM_WIDTH)
        if i % 2 == 0:
            o = _window_attention(tok, kt, kmt, vm, lw["sink"], lw["bias"], tq=512)
        else:
            o = _nbr_attention(tok, kt, kmt, vm, lw["bias"])
        x, h = _out_proj(x, o.reshape(n, MIX_WIDTH), lw["w_o"], lw["g_ffn"], tm)
        x = _conv_ffn(x.reshape(b, t, d), h.reshape(b, t, d), lw["w_gate"], lw["w_up"], lw["conv_w"],
                      lw["conv_b"], lw["w_down"], g_final, tm, final_norm=(i == DEPTH - 1)).reshape(n, d)
    return x.reshape(b, t, d)


def kernel(x_prompt, x_sample, mem_prompt, mem_sample, g_mix, g_mem, w_in_a, sink_a, w_in_b, rpb_b, w_mem_kv, w_o,
           g_ffn, w_gate, w_up, conv_w, conv_b, w_down, g_final):
    layers = _prepare_weights(g_mix, g_mem, w_in_a, sink_a, w_in_b, rpb_b, w_mem_kv, w_o, g_ffn, w_gate, w_up,
                              conv_w, conv_b, w_down, g_final)
    y_prompt = _trunk(x_prompt, mem_prompt, layers, g_final)
    y_sample = _trunk(x_sample, mem_sample, layers, g_final)
    return (y_prompt, y_sample)
```

```python
import functools

import numpy as np
import jax
import jax.numpy as jnp
from jax import lax
from jax.experimental import pallas as pl
from jax.experimental.pallas import tpu as pltpu

D_MODEL = 1024
DEPTH = 2
HEAD_DIM = 64
N_Q_HEADS = 12
N_KV_HEADS_A = 4
N_MEM_HEADS = 4
MEM_LEN = 256
WINDOW = 128
GRID_W = 64
NA_ROWS = 8
NA_COLS = 16
D_FF = 2816
CONV_W = 3
EPS = 1e-6
NEG = -1e30
Q_WIDTH = N_Q_HEADS * HEAD_DIM
KV_WIDTH_A = N_KV_HEADS_A * HEAD_DIM
MEM_WIDTH = N_MEM_HEADS * HEAD_DIM
MIX_WIDTH = Q_WIDTH + MEM_WIDTH
LOG2E = 1.4426950408889634
Q_SCALE = HEAD_DIM ** -0.5 * LOG2E
REP_A = N_Q_HEADS // N_KV_HEADS_A

F32 = jnp.float32
BF16 = jnp.bfloat16

LANES = 128
VMEM_LIMIT_BYTES = 56 * 1024 * 1024
FF_CHUNK = 256
HALO = 16

_NT = (((1,), (1,)), ((), ()))


def _params(*sem):
    return pltpu.CompilerParams(dimension_semantics=sem, vmem_limit_bytes=VMEM_LIMIT_BYTES)


def _resident(shape, index_map):
    return pl.BlockSpec(shape, index_map, pipeline_mode=pl.Buffered(1))


def _rms(x, g):
    return x * lax.rsqrt(jnp.mean(x * x, axis=-1, keepdims=True) + EPS) * g


def _proj_kernel(x_ref, g_ref, w_ref, cs_ref, wt_ref, tok_ref, feat_ref):
    h = _rms(x_ref[...], g_ref[...]).astype(BF16)
    tok = jnp.dot(h, w_ref[...], preferred_element_type=F32) * cs_ref[...]
    tok_ref[...] = tok.astype(tok_ref.dtype)
    feat_ref[0] = lax.dot_general(wt_ref[...], h, _NT, preferred_element_type=F32).astype(feat_ref.dtype)


def _norm_project(x, g, w_tok, col_scale, w_feat_t, batch, tm):
    n, d = x.shape
    t = n // batch
    per_seq = t // tm
    wdt = w_tok.shape[1]
    wft = w_feat_t.shape[0]
    return pl.pallas_call(
        _proj_kernel,
        grid=(n // tm,),
        in_specs=[
            pl.BlockSpec((tm, d), lambda i: (i, 0)),
            _resident((1, d), lambda i: (0, 0)),
            _resident((d, wdt), lambda i: (0, 0)),
            _resident((1, wdt), lambda i: (0, 0)),
            _resident((wft, d), lambda i: (0, 0)),
        ],
        out_specs=[pl.BlockSpec((tm, wdt), lambda i: (i, 0)),
                   pl.BlockSpec((1, wft, tm), lambda i: (i // per_seq, 0, i % per_seq))],
        out_shape=[jax.ShapeDtypeStruct((n, wdt), BF16), jax.ShapeDtypeStruct((batch, wft, t), BF16)],
        compiler_params=_params("parallel"),
        name="norm_project",
    )(x, g.reshape(1, d), w_tok, col_scale.reshape(1, wdt), w_feat_t)


def _low_lanes(rows):
    return lax.broadcasted_iota(jnp.int32, (rows, LANES), 1) < HEAD_DIM


def _split_heads(q_tile, low):
    zero = jnp.zeros_like(q_tile)
    return jnp.where(low, q_tile, zero), jnp.where(low, zero, q_tile)


def _with_ones(v):
    return jnp.concatenate([v, jnp.ones_like(v)], axis=1)


def _merge_heads(oa, low, lo_rows, hi_rows, extra_den=None):
    pick = lambda a: jnp.where(low, a[lo_rows], a[hi_rows])
    den = pick(oa[:, LANES:])
    if extra_den is not None:
        den = den + pick(extra_den)
    return pick(oa[:, :LANES]) / den


def _memory_attention(qm_ref, kmt_ref, vm_ref, o_ref):
    tq = qm_ref.shape[1]
    low = _low_lanes(tq)
    for p in range(MEM_WIDTH // LANES):
        cols = slice(p * LANES, (p + 1) * LANES)
        qs = jnp.concatenate(_split_heads(qm_ref[0, :, cols], low), axis=0)
        s = jnp.dot(qs, kmt_ref[0, cols, :], preferred_element_type=F32)
        pr = jnp.exp2(s - jnp.max(s, axis=-1, keepdims=True)).astype(BF16)
        oa = jnp.dot(pr, _with_ones(vm_ref[0, :, cols]), preferred_element_type=F32)
        o_ref[0, :, Q_WIDTH + p * LANES:Q_WIDTH + (p + 1) * LANES] = (
            _merge_heads(oa, low, slice(0, tq), slice(tq, 2 * tq)).astype(o_ref.dtype))


def _window_head(pair, row_block):
    return REP_A * (2 * pair + row_block % 2) + row_block // 2


def _window_head_order():
    return [_window_head(p, rb) for p in range(N_KV_HEADS_A // 2) for rb in range(2 * REP_A)]


def _window_bias_table():
    slopes = (2.0 ** (-8.0 * np.arange(1, N_Q_HEADS + 1, dtype=np.float32) / N_Q_HEADS)).astype(np.float32)
    blk = WINDOW
    qq = np.arange(blk)[:, None]
    kk = np.arange(3 * blk)[None, :]
    arel = np.abs(qq - kk + WINDOW)
    in_window = arel <= WINDOW
    in_seq = [kk >= blk, np.ones_like(kk, bool), kk < 2 * blk]
    out = np.empty((3, N_KV_HEADS_A // 2, 2 * REP_A, blk, 3 * blk), np.float32)
    for var in range(3):
        for p in range(N_KV_HEADS_A // 2):
            for rb in range(2 * REP_A):
                alibi = -slopes[_window_head(p, rb)] * arel.astype(np.float32) * np.float32(LOG2E)
                out[var, p, rb] = np.where(in_window & in_seq[var], alibi, np.float32(NEG))
    return out.reshape(3 * (N_KV_HEADS_A // 2), 2 * REP_A * blk, 3 * blk)


def _window_kernel(sink_ref, q_ref, ktp_ref, ktc_ref, ktn_ref, vp_ref, vc_ref, vn_ref, qm_ref, kmt_ref, vm_ref,
                   bias_ref, o_ref, ktbuf, vbuf, *, tq, n_blk):
    i = pl.program_id(1)
    blk = WINDOW
    n_pair = N_KV_HEADS_A // 2
    ktbuf[:, 0:blk] = ktp_ref[0]
    ktbuf[:, blk:blk + tq] = ktc_ref[0]
    ktbuf[:, blk + tq:] = ktn_ref[0]
    vbuf[0:blk] = vp_ref[0]
    vbuf[blk:blk + tq] = vc_ref[0]
    vbuf[blk + tq:] = vn_ref[0]

    low = _low_lanes(blk)
    sinks = [jnp.concatenate([jnp.full((blk, LANES), sink_ref[_window_head(p, rb)], F32)
                              for rb in range(2 * REP_A)], axis=0) for p in range(n_pair)]
    for j in range(tq // blk):
        g_blk = i * (tq // blk) + j
        variant = jnp.where(g_blk == 0, 0, jnp.where(g_blk == n_blk - 1, 2, 1))
        rows = slice(j * blk, (j + 1) * blk)
        keys = slice(j * blk, j * blk + 3 * blk)
        for p in range(n_pair):
            tiles = []
            for t in range(REP_A):
                c0 = (REP_A * p + t) * LANES
                tiles.extend(_split_heads(q_ref[0, rows, c0:c0 + LANES], low))
            qs = jnp.concatenate(tiles, axis=0)
            sc = jnp.dot(qs, ktbuf[p * LANES:(p + 1) * LANES, keys], preferred_element_type=F32)
            sc = sc + bias_ref[variant * n_pair + p]
            m = jnp.maximum(jnp.max(sc, axis=-1, keepdims=True), sinks[p])
            pr = jnp.concatenate([jnp.exp2(sc[:, c * LANES:(c + 1) * LANES] - m) for c in range(3 * blk // LANES)],
                                 axis=1).astype(BF16)
            oa = jnp.dot(pr, _with_ones(vbuf[keys, p * LANES:(p + 1) * LANES]), preferred_element_type=F32)
            sink_mass = jnp.exp2(sinks[p] - m)
            for t in range(REP_A):
                c0 = (REP_A * p + t) * LANES
                o_ref[0, rows, c0:c0 + LANES] = _merge_heads(
                    oa, low, slice(2 * t * blk, (2 * t + 1) * blk), slice((2 * t + 1) * blk, (2 * t + 2) * blk),
                    extra_den=sink_mass).astype(o_ref.dtype)

    _memory_attention(qm_ref, kmt_ref, vm_ref, o_ref)


def _window_attention(tok, kt, kmt, vm, sink, bias, tq):
    b, t, _ = tok.shape
    blk = WINDOW
    n_blk = t // blk
    per = tq // blk
    assert n_blk >= 2 and t % tq == 0
    vcol = Q_WIDTH // KV_WIDTH_A
    prev_i = lambda i: jnp.maximum(i * per - 1, 0)
    next_i = lambda i: jnp.minimum((i + 1) * per, n_blk - 1)
    kernel = functools.partial(_window_kernel, tq=tq, n_blk=n_blk)
    return pl.pallas_call(
        kernel,
        grid=(b, t // tq),
        in_specs=[
            pl.BlockSpec(memory_space=pltpu.SMEM),
            pl.BlockSpec((1, tq, Q_WIDTH), lambda bi, i: (bi, i, 0)),
            pl.BlockSpec((1, KV_WIDTH_A, blk), lambda bi, i: (bi, 0, prev_i(i))),
            pl.BlockSpec((1, KV_WIDTH_A, tq), lambda bi, i: (bi, 0, i)),
            pl.BlockSpec((1, KV_WIDTH_A, blk), lambda bi, i: (bi, 0, next_i(i))),
            pl.BlockSpec((1, blk, KV_WIDTH_A), lambda bi, i: (bi, prev_i(i), vcol)),
            pl.BlockSpec((1, tq, KV_WIDTH_A), lambda bi, i: (bi, i, vcol)),
            pl.BlockSpec((1, blk, KV_WIDTH_A), lambda bi, i: (bi, next_i(i), vcol)),
            pl.BlockSpec((1, tq, MEM_WIDTH), lambda bi, i: (bi, i, vcol + 1)),
            pl.BlockSpec((1, MEM_WIDTH, MEM_LEN), lambda bi, i: (bi, 0, 0)),
            pl.BlockSpec((1, MEM_LEN, MEM_WIDTH), lambda bi, i: (bi, 0, 0)),
            _resident(bias.shape, lambda bi, i: (0, 0, 0)),
        ],
        out_specs=pl.BlockSpec((1, tq, MIX_WIDTH), lambda bi, i: (bi, i, 0)),
        out_shape=jax.ShapeDtypeStruct((b, t, MIX_WIDTH), BF16),
        scratch_shapes=[pltpu.VMEM((KV_WIDTH_A, tq + 2 * blk), BF16),
                        pltpu.VMEM((tq + 2 * blk, KV_WIDTH_A), BF16)],
        compiler_params=_params("parallel", "parallel"),
        name="window_attention",
    )(sink, tok, kt, kt, kt, tok, tok, tok, tok, kmt, vm, bias)


NB_ROWS_PER_TILE = 4
NB_TQ = NB_ROWS_PER_TILE * GRID_W


def _nbr_bias_table(rpb):
    h, n_dr, n_dc = rpb.shape
    w = GRID_W
    lo = (w - 1) - (NA_COLS - 1)
    e = jnp.pad(rpb.astype(F32), ((0, 0), (0, 0), (lo, 2 * w - lo - n_dc)))
    toe = jnp.tile(e, (1, 1, w))[..., :w * (2 * w - 1)].reshape(h, n_dr, w, 2 * w - 1)
    blocks = toe[..., w - 1:]
    zero = jnp.zeros((h, w, w), F32)
    rows = []
    for a in range(NB_ROWS_PER_TILE):
        strip = []
        for b in range(3 * NB_ROWS_PER_TILE):
            dr = (b - NB_ROWS_PER_TILE) - a + (NA_ROWS - 1)
            strip.append(blocks[:, dr] if 0 <= dr < n_dr else zero)
        rows.append(jnp.concatenate(strip, axis=-1))
    return jnp.concatenate(rows, axis=1)


def _nbr_mask_table(n_tiles, n_rows):
    qi = np.arange(NB_TQ)[:, None]
    ki = np.arange(3 * NB_TQ)[None, :]
    c0 = np.clip(qi % GRID_W - NA_COLS // 2, 0, GRID_W - NA_COLS)
    k_col = ki % GRID_W
    col_ok = (k_col >= c0) & (k_col < c0 + NA_COLS)
    out = []
    for i in (0, min(1, n_tiles - 1), n_tiles - 1):
        q_row = i * NB_ROWS_PER_TILE + qi // GRID_W
        k_row = (i - 1) * NB_ROWS_PER_TILE + ki // GRID_W
        r0 = np.clip(q_row - NA_ROWS // 2, 0, n_rows - NA_ROWS)
        row_ok = (k_row >= r0) & (k_row < r0 + NA_ROWS)
        out.append(np.where(row_ok & col_ok, np.float32(0.0), np.float32(NEG)))
    return np.stack(out).astype(np.float32)


_NBR_CANONICAL_TILES = 3


def _nbr_logit_table(rpb):
    mask = _nbr_mask_table(_NBR_CANONICAL_TILES, _NBR_CANONICAL_TILES * NB_ROWS_PER_TILE)
    table = (_nbr_bias_table(rpb)[None] + jnp.asarray(mask)[:, None]) * LOG2E
    return table.reshape(3 * N_Q_HEADS, NB_TQ, 3 * NB_TQ)


def _nbr_kernel(q_ref, ktp_ref, ktc_ref, ktn_ref, vp_ref, vc_ref, vn_ref, qm_ref, kmt_ref, vm_ref, table_ref,
                o_ref, *, n_tiles):
    i = pl.program_id(1)
    tq = NB_TQ
    low = _low_lanes(tq)
    variant = jnp.where(i == 0, 0, jnp.where(i == n_tiles - 1, 2, 1))
    for p in range(Q_WIDTH // LANES):
        cols = slice(p * LANES, (p + 1) * LANES)
        qs = jnp.concatenate(_split_heads(q_ref[0, :, cols], low), axis=0)
        kt = jnp.concatenate([ktp_ref[0, cols, :], ktc_ref[0, cols, :], ktn_ref[0, cols, :]], axis=1)
        sc = jnp.dot(qs, kt, preferred_element_type=F32)
        sc = sc + jnp.concatenate([table_ref[variant * N_Q_HEADS + 2 * p],
                                   table_ref[variant * N_Q_HEADS + 2 * p + 1]], axis=0)
        pr = jnp.exp2(sc - jnp.max(sc, axis=-1, keepdims=True)).astype(BF16)
        v = jnp.concatenate([vp_ref[0, :, cols], vc_ref[0, :, cols], vn_ref[0, :, cols]], axis=0)
        oa = jnp.dot(pr, _with_ones(v), preferred_element_type=F32)
        o_ref[0, :, cols] = _merge_heads(oa, low, slice(0, tq), slice(tq, 2 * tq)).astype(o_ref.dtype)

    _memory_attention(qm_ref, kmt_ref, vm_ref, o_ref)


def _nbr_attention(tok, kt, kmt, vm, table):
    b, t, _ = tok.shape
    tq = NB_TQ
    n = t // tq
    n_rows = t // GRID_W
    assert n_rows >= NA_ROWS and t % tq == 0 and n >= 2
    canonical = _nbr_mask_table(_NBR_CANONICAL_TILES, _NBR_CANONICAL_TILES * NB_ROWS_PER_TILE)
    assert n == 2 or np.array_equal(_nbr_mask_table(n, n_rows), canonical)
    assert n > 2 or np.array_equal(_nbr_mask_table(n, n_rows)[[0, 2]], canonical[[0, 2]])
    prev_i = lambda i: jnp.maximum(i - 1, 0)
    next_i = lambda i: jnp.minimum(i + 1, n - 1)
    kt_spec = lambda f: pl.BlockSpec((1, Q_WIDTH, tq), lambda bi, i: (bi, 0, f(i)))
    v_spec = lambda f: pl.BlockSpec((1, tq, Q_WIDTH), lambda bi, i: (bi, f(i), 1))
    same = lambda i: i
    return pl.pallas_call(
        functools.partial(_nbr_kernel, n_tiles=n),
        grid=(b, n),
        in_specs=[
            pl.BlockSpec((1, tq, Q_WIDTH), lambda bi, i: (bi, i, 0)),
            kt_spec(prev_i), kt_spec(same), kt_spec(next_i),
            v_spec(prev_i), v_spec(same), v_spec(next_i),
            pl.BlockSpec((1, tq, MEM_WIDTH), lambda bi, i: (bi, i, 2 * Q_WIDTH // MEM_WIDTH)),
            pl.BlockSpec((1, MEM_WIDTH, MEM_LEN), lambda bi, i: (bi, 0, 0)),
            pl.BlockSpec((1, MEM_LEN, MEM_WIDTH), lambda bi, i: (bi, 0, 0)),
            _resident(table.shape, lambda bi, i: (0, 0, 0)),
        ],
        out_specs=pl.BlockSpec((1, tq, MIX_WIDTH), lambda bi, i: (bi, i, 0)),
        out_shape=jax.ShapeDtypeStruct((b, t, MIX_WIDTH), BF16),
        compiler_params=_params("parallel", "parallel"),
        name="nbr_attention",
    )(tok, kt, kt, kt, tok, tok, tok, tok, kmt, vm, table)


def _post_kernel(xp_ref, xc_ref, xn_ref, op_ref, oc_ref, on_ref, wo_ref, gn_ref, wg_ref, wu_ref, cw_ref, cb_ref,
                 wd_ref, gf_ref, y_ref, oext_ref, hext_ref, act_ref, *, final_norm):
    i = pl.program_id(1)
    n = pl.num_programs(1)
    tm = xc_ref.shape[1]
    ext = tm + 2 * HALO
    oext_ref[0:HALO] = op_ref[0]
    oext_ref[HALO:HALO + tm] = oc_ref[0]
    oext_ref[HALO + tm:] = on_ref[0]
    mix = jnp.dot(oext_ref[...], wo_ref[...], preferred_element_type=F32)
    gn = gn_ref[...]
    x1 = xc_ref[0] + mix[HALO:HALO + tm]
    y_ref[0] = x1
    hext_ref[HALO:HALO + tm] = _rms(x1, gn).astype(BF16)
    h_before = _rms(xp_ref[0] + mix[0:HALO], gn).astype(BF16)
    h_after = _rms(xn_ref[0] + mix[HALO + tm:], gn).astype(BF16)
    hext_ref[0:HALO] = jnp.where(i > 0, h_before, jnp.zeros_like(h_before))
    hext_ref[HALO + tm:] = jnp.where(i < n - 1, h_after, jnp.zeros_like(h_after))

    for c in range(wg_ref.shape[0]):
        g_ext = jnp.dot(hext_ref[...], wg_ref[c], preferred_element_type=F32)
        g = g_ext[HALO:HALO + tm]
        g_prev = pltpu.roll(g_ext, 1, 0)[HALO:HALO + tm]
        g_next = pltpu.roll(g_ext, ext - 1, 0)[HALO:HALO + tm]
        cw = cw_ref[c]
        gc = g_prev * cw[0:1] + g * cw[1:2] + g_next * cw[2:3] + cb_ref[c]
        u = jnp.dot(hext_ref[HALO:HALO + tm], wu_ref[c], preferred_element_type=F32)
        act_ref[:, c * FF_CHUNK:(c + 1) * FF_CHUNK] = (gc * jax.nn.sigmoid(gc) * u).astype(BF16)

    y = y_ref[0] + jnp.dot(act_ref[...], wd_ref[...], preferred_element_type=F32)
    if final_norm:
        y = _rms(y, gf_ref[...])
    y_ref[0] = y


def _post_attention(x, o, w_o, g_ffn, w_gate, w_up, conv_w, conv_b, w_down, g_final, tm, final_norm):
    b, t, d = x.shape
    per = tm // HALO
    n_halo = t // HALO
    kernel = functools.partial(_post_kernel, final_norm=final_norm)
    const3 = lambda bi, i: (0, 0, 0)
    const2 = lambda bi, i: (0, 0)
    prev_map = lambda bi, i: (bi, jnp.maximum(i * per - 1, 0), 0)
    cur_map = lambda bi, i: (bi, i, 0)
    next_map = lambda bi, i: (bi, jnp.minimum((i + 1) * per, n_halo - 1), 0)
    return pl.pallas_call(
        kernel,
        grid=(b, t // tm),
        in_specs=[
            pl.BlockSpec((1, HALO, d), prev_map),
            pl.BlockSpec((1, tm, d), cur_map),
            pl.BlockSpec((1, HALO, d), next_map),
            pl.BlockSpec((1, HALO, MIX_WIDTH), prev_map),
            pl.BlockSpec((1, tm, MIX_WIDTH), cur_map),
            pl.BlockSpec((1, HALO, MIX_WIDTH), next_map),
            _resident(w_o.shape, const2),
            _resident((1, d), const2),
            _resident(w_gate.shape, const3),
            _resident(w_up.shape, const3),
            _resident(conv_w.shape, const3),
            _resident(conv_b.shape, const3),
            _resident(w_down.shape, const2),
            _resident((1, d), const2),
        ],
        out_specs=pl.BlockSpec((1, tm, d), cur_map),
        out_shape=jax.ShapeDtypeStruct((b, t, d), F32),
        scratch_shapes=[pltpu.VMEM((tm + 2 * HALO, MIX_WIDTH), BF16), pltpu.VMEM((tm + 2 * HALO, d), BF16),
                        pltpu.VMEM((tm, D_FF), BF16)],
        compiler_params=_params("parallel", "parallel"),
        name="post_attention",
    )(x, x, x, o, o, o, w_o, g_ffn.reshape(1, d), w_gate, w_up, conv_w, conv_b, w_down, g_final.reshape(1, d))


def _chunk_cols(w):
    d = w.shape[0]
    return w.reshape(d, D_FF // FF_CHUNK, FF_CHUNK).transpose(1, 0, 2)


def _prepare_weights(g_mix, g_mem, w_in_a, sink_a, w_in_b, rpb_b, w_mem_kv, w_o, g_ffn, w_gate, w_up, conv_w,
                     conv_b, w_down, g_final):
    nc = D_FF // FF_CHUNK
    head_cols = np.arange(Q_WIDTH).reshape(N_Q_HEADS, HEAD_DIM)
    perm_a = head_cols[_window_head_order()].reshape(-1)
    layers = []
    for i in range(DEPTH):
        j = i // 2
        wo = w_o[i]
        if i % 2 == 0:
            w = w_in_a[j]
            wq, wk, wv, wqm = (w[:, :Q_WIDTH], w[:, Q_WIDTH:Q_WIDTH + KV_WIDTH_A],
                               w[:, Q_WIDTH + KV_WIDTH_A:Q_WIDTH + 2 * KV_WIDTH_A], w[:, Q_WIDTH + 2 * KV_WIDTH_A:])
            wq = wq[:, perm_a]
            wo = jnp.concatenate([wo[:Q_WIDTH][perm_a], wo[Q_WIDTH:]], axis=0)
            extra = dict(sink=sink_a[j].astype(F32) * LOG2E, bias=jnp.asarray(_window_bias_table()))
        else:
            w = w_in_b[j]
            wq, wk, wv, wqm = (w[:, :Q_WIDTH], w[:, Q_WIDTH:2 * Q_WIDTH], w[:, 2 * Q_WIDTH:3 * Q_WIDTH],
                               w[:, 3 * Q_WIDTH:])
            extra = dict(bias=_nbr_logit_table(rpb_b[j]))
        tok_scale = np.concatenate([np.full(wq.shape[1], Q_SCALE), np.ones(wv.shape[1]),
                                    np.full(wqm.shape[1], Q_SCALE)]).astype(np.float32)
        layers.append(dict(
            g_mix=g_mix[i], g_mem=g_mem[i], g_ffn=g_ffn[i],
            w_tok=jnp.concatenate([wq, wv, wqm], axis=1).astype(BF16), tok_scale=jnp.asarray(tok_scale),
            w_kt=wk.T.astype(BF16),
            w_vm=w_mem_kv[i][:, MEM_WIDTH:].astype(BF16), w_kmt=w_mem_kv[i][:, :MEM_WIDTH].T.astype(BF16),
            w_o=wo.astype(BF16),
            w_gate=_chunk_cols(w_gate[i]).astype(BF16), w_up=_chunk_cols(w_up[i]).astype(BF16),
            conv_w=conv_w[i].reshape(CONV_W, nc, FF_CHUNK).transpose(1, 0, 2),
            conv_b=conv_b[i].reshape(nc, 1, FF_CHUNK),
            w_down=w_down[i].astype(BF16), **extra))
    return layers


def _trunk(x, mem, layers, g_final):
    b, t, d = x.shape
    n = b * t
    tm = 512
    mem2 = mem.reshape(b * MEM_LEN, d)
    ones_mem = jnp.ones((MEM_WIDTH,), F32)
    for i, lw in enumerate(layers):
        tok, kt = _norm_project(x.reshape(n, d), lw["g_mix"], lw["w_tok"], lw["tok_scale"], lw["w_kt"], b, tm)
        vm, kmt = _norm_project(mem2, lw["g_mem"], lw["w_vm"], ones_mem, lw["w_kmt"], b, MEM_LEN)
        tok = tok.reshape(b, t, -1)
        vm = vm.reshape(b, MEM_LEN, MEM_WIDTH)
        if i % 2 == 0:
            o = _window_attention(tok, kt, kmt, vm, lw["sink"], lw["bias"], tq=512)
        else:
            o = _nbr_attention(tok, kt, kmt, vm, lw["bias"])
        x = _post_attention(x, o, lw["w_o"], lw["g_ffn"], lw["w_gate"], lw["w_up"], lw["conv_w"], lw["conv_b"],
                            lw["w_down"], g_final, tm, final_norm=(i == DEPTH - 1))
    return x


def kernel(x_prompt, x_sample, mem_prompt, mem_sample, g_mix, g_mem, w_in_a, sink_a, w_in_b, rpb_b, w_mem_kv, w_o,
           g_ffn, w_gate, w_up, conv_w, conv_b, w_down, g_final):
    layers = _prepare_weights(g_mix, g_mem, w_in_a, sink_a, w_in_b, rpb_b, w_mem_kv, w_o, g_ffn, w_gate, w_up,
                              conv_w, conv_b, w_down, g_final)
    y_prompt = _trunk(x_prompt, mem_prompt, layers, g_final)
    y_sample = _trunk(x_sample, mem_sample, layers, g_final)
    return (y_prompt, y_sample)
```

```python
import functools

import numpy as np
import jax
import jax.numpy as jnp
from jax import lax
from jax.experimental import pallas as pl
from jax.experimental.pallas import tpu as pltpu

D_MODEL = 1024
DEPTH = 2
HEAD_DIM = 64
N_Q_HEADS = 12
N_KV_HEADS_A = 4
N_MEM_HEADS = 4
MEM_LEN = 256
WINDOW = 128
GRID_W = 64
NA_ROWS = 8
NA_COLS = 16
D_FF = 2816
CONV_W = 3
EPS = 1e-6
NEG = -1e30
Q_WIDTH = N_Q_HEADS * HEAD_DIM
KV_WIDTH_A = N_KV_HEADS_A * HEAD_DIM
MEM_WIDTH = N_MEM_HEADS * HEAD_DIM
MIX_WIDTH = Q_WIDTH + MEM_WIDTH
LOG2E = 1.4426950408889634
Q_SCALE = HEAD_DIM ** -0.5 * LOG2E
REP_A = N_Q_HEADS // N_KV_HEADS_A

F32 = jnp.float32
BF16 = jnp.bfloat16

LANES = 128
VMEM_LIMIT_BYTES = 56 * 1024 * 1024
FF_CHUNK = 256
HALO = 16

_NT = (((1,), (1,)), ((), ()))


def _params(*sem):
    return pltpu.CompilerParams(dimension_semantics=sem, vmem_limit_bytes=VMEM_LIMIT_BYTES)


def _resident(shape, index_map):
    return pl.BlockSpec(shape, index_map, pipeline_mode=pl.Buffered(1))


def _rms(x, g):
    return x * lax.rsqrt(jnp.mean(x * x, axis=-1, keepdims=True) + EPS) * g


def _proj_kernel(x_ref, g_ref, w_ref, cs_ref, wt_ref, tok_ref, feat_ref):
    half = x_ref.shape[0] // 2
    for r in range(2):
        rows = slice(r * half, (r + 1) * half)
        h = _rms(x_ref[rows, :], g_ref[...]).astype(BF16)
        tok = jnp.dot(h, w_ref[...], preferred_element_type=F32) * cs_ref[...]
        tok_ref[rows, :] = tok.astype(tok_ref.dtype)
        feat_ref[0, :, rows] = lax.dot_general(wt_ref[...], h, _NT, preferred_element_type=F32).astype(feat_ref.dtype)


def _norm_project(x, g, w_tok, col_scale, w_feat_t, batch, tm):
    n, d = x.shape
    t = n // batch
    per_seq = t // tm
    wdt = w_tok.shape[1]
    wft = w_feat_t.shape[0]
    return pl.pallas_call(
        _proj_kernel,
        grid=(n // tm,),
        in_specs=[
            pl.BlockSpec((tm, d), lambda i: (i, 0)),
            _resident((1, d), lambda i: (0, 0)),
            _resident((d, wdt), lambda i: (0, 0)),
            _resident((1, wdt), lambda i: (0, 0)),
            _resident((wft, d), lambda i: (0, 0)),
        ],
        out_specs=[pl.BlockSpec((tm, wdt), lambda i: (i, 0)),
                   pl.BlockSpec((1, wft, tm), lambda i: (i // per_seq, 0, i % per_seq))],
        out_shape=[jax.ShapeDtypeStruct((n, wdt), BF16), jax.ShapeDtypeStruct((batch, wft, t), BF16)],
        compiler_params=_params("parallel"),
        name="norm_project",
    )(x, g.reshape(1, d), w_tok, col_scale.reshape(1, wdt), w_feat_t)


def _low_lanes(rows):
    return lax.broadcasted_iota(jnp.int32, (rows, LANES), 1) < HEAD_DIM


def _split_heads(q_tile, low):
    zero = jnp.zeros_like(q_tile)
    return jnp.where(low, q_tile, zero), jnp.where(low, zero, q_tile)


def _with_ones(v):
    return jnp.concatenate([v, jnp.ones_like(v)], axis=1)


def _merge_heads(oa, low, lo_rows, hi_rows, extra_den=None):
    pick = lambda a: jnp.where(low, a[lo_rows], a[hi_rows])
    den = pick(oa[:, LANES:])
    if extra_den is not None:
        den = den + pick(extra_den)
    return pick(oa[:, :LANES]) / den


def _memory_attention(qm_ref, kmt_ref, vm_ref, o_ref):
    tq = qm_ref.shape[1]
    low = _low_lanes(tq)
    for p in range(MEM_WIDTH // LANES):
        cols = slice(p * LANES, (p + 1) * LANES)
        qs = jnp.concatenate(_split_heads(qm_ref[0, :, cols], low), axis=0)
        s = jnp.dot(qs, kmt_ref[0, cols, :], preferred_element_type=F32)
        pr = jnp.exp2(s - jnp.max(s, axis=-1, keepdims=True)).astype(BF16)
        oa = jnp.dot(pr, _with_ones(vm_ref[0, :, cols]), preferred_element_type=F32)
        o_ref[0, :, Q_WIDTH + p * LANES:Q_WIDTH + (p + 1) * LANES] = (
            _merge_heads(oa, low, slice(0, tq), slice(tq, 2 * tq)).astype(o_ref.dtype))


def _window_head(pair, row_block):
    return REP_A * (2 * pair + row_block % 2) + row_block // 2


def _window_head_order():
    return [_window_head(p, rb) for p in range(N_KV_HEADS_A // 2) for rb in range(2 * REP_A)]


def _window_bias_table():
    slopes = (2.0 ** (-8.0 * np.arange(1, N_Q_HEADS + 1, dtype=np.float32) / N_Q_HEADS)).astype(np.float32)
    blk = WINDOW
    qq = np.arange(blk)[:, None]
    kk = np.arange(3 * blk)[None, :]
    arel = np.abs(qq - kk + WINDOW)
    in_window = arel <= WINDOW
    in_seq = [kk >= blk, np.ones_like(kk, bool), kk < 2 * blk]
    out = np.empty((3, N_KV_HEADS_A // 2, 2 * REP_A, blk, 3 * blk), np.float32)
    for var in range(3):
        for p in range(N_KV_HEADS_A // 2):
            for rb in range(2 * REP_A):
                alibi = -slopes[_window_head(p, rb)] * arel.astype(np.float32) * np.float32(LOG2E)
                out[var, p, rb] = np.where(in_window & in_seq[var], alibi, np.float32(NEG))
    return out.reshape(3 * (N_KV_HEADS_A // 2), 2 * REP_A * blk, 3 * blk)


def _window_kernel(sink_ref, q_ref, ktp_ref, ktc_ref, ktn_ref, vp_ref, vc_ref, vn_ref, qm_ref, kmt_ref, vm_ref,
                   bias_ref, o_ref, ktbuf, vbuf, *, tq, n_blk):
    i = pl.program_id(1)
    blk = WINDOW
    n_pair = N_KV_HEADS_A // 2
    ktbuf[:, 0:blk] = ktp_ref[0]
    ktbuf[:, blk:blk + tq] = ktc_ref[0]
    ktbuf[:, blk + tq:] = ktn_ref[0]
    vbuf[0:blk] = vp_ref[0]
    vbuf[blk:blk + tq] = vc_ref[0]
    vbuf[blk + tq:] = vn_ref[0]

    low = _low_lanes(blk)
    sinks = [jnp.concatenate([jnp.full((blk, LANES), sink_ref[_window_head(p, rb)], F32)
                              for rb in range(2 * REP_A)], axis=0) for p in range(n_pair)]
    for j in range(tq // blk):
        g_blk = i * (tq // blk) + j
        variant = jnp.where(g_blk == 0, 0, jnp.where(g_blk == n_blk - 1, 2, 1))
        rows = slice(j * blk, (j + 1) * blk)
        keys = slice(j * blk, j * blk + 3 * blk)
        for p in range(n_pair):
            tiles = []
            for t in range(REP_A):
                c0 = (REP_A * p + t) * LANES
                tiles.extend(_split_heads(q_ref[0, rows, c0:c0 + LANES], low))
            qs = jnp.concatenate(tiles, axis=0)
            sc = jnp.dot(qs, ktbuf[p * LANES:(p + 1) * LANES, keys], preferred_element_type=F32)
            sc = sc + bias_ref[variant * n_pair + p]
            m = jnp.maximum(jnp.max(sc, axis=-1, keepdims=True), sinks[p])
            pr = jnp.concatenate([jnp.exp2(sc[:, c * LANES:(c + 1) * LANES] - m) for c in range(3 * blk // LANES)],
                                 axis=1).astype(BF16)
            oa = jnp.dot(pr, _with_ones(vbuf[keys, p * LANES:(p + 1) * LANES]), preferred_element_type=F32)
            sink_mass = jnp.exp2(sinks[p] - m)
            for t in range(REP_A):
                c0 = (REP_A * p + t) * LANES
                o_ref[0, rows, c0:c0 + LANES] = _merge_heads(
                    oa, low, slice(2 * t * blk, (2 * t + 1) * blk), slice((2 * t + 1) * blk, (2 * t + 2) * blk),
                    extra_den=sink_mass).astype(o_ref.dtype)

    _memory_attention(qm_ref, kmt_ref, vm_ref, o_ref)


def _window_attention(tok, kt, kmt, vm, sink, bias, tq):
    b, t, _ = tok.shape
    blk = WINDOW
    n_blk = t // blk
    per = tq // blk
    assert n_blk >= 2 and t % tq == 0
    vcol = Q_WIDTH // KV_WIDTH_A
    prev_i = lambda i: jnp.maximum(i * per - 1, 0)
    next_i = lambda i: jnp.minimum((i + 1) * per, n_blk - 1)
    kernel = functools.partial(_window_kernel, tq=tq, n_blk=n_blk)
    return pl.pallas_call(
        kernel,
        grid=(b, t // tq),
        in_specs=[
            pl.BlockSpec(memory_space=pltpu.SMEM),
            pl.BlockSpec((1, tq, Q_WIDTH), lambda bi, i: (bi, i, 0)),
            pl.BlockSpec((1, KV_WIDTH_A, blk), lambda bi, i: (bi, 0, prev_i(i))),
            pl.BlockSpec((1, KV_WIDTH_A, tq), lambda bi, i: (bi, 0, i)),
            pl.BlockSpec((1, KV_WIDTH_A, blk), lambda bi, i: (bi, 0, next_i(i))),
            pl.BlockSpec((1, blk, KV_WIDTH_A), lambda bi, i: (bi, prev_i(i), vcol)),
            pl.BlockSpec((1, tq, KV_WIDTH_A), lambda bi, i: (bi, i, vcol)),
            pl.BlockSpec((1, blk, KV_WIDTH_A), lambda bi, i: (bi, next_i(i), vcol)),
            pl.BlockSpec((1, tq, MEM_WIDTH), lambda bi, i: (bi, i, vcol + 1)),
            pl.BlockSpec((1, MEM_WIDTH, MEM_LEN), lambda bi, i: (bi, 0, 0)),
            pl.BlockSpec((1, MEM_LEN, MEM_WIDTH), lambda bi, i: (bi, 0, 0)),
            _resident(bias.shape, lambda bi, i: (0, 0, 0)),
        ],
        out_specs=pl.BlockSpec((1, tq, MIX_WIDTH), lambda bi, i: (bi, i, 0)),
        out_shape=jax.ShapeDtypeStruct((b, t, MIX_WIDTH), BF16),
        scratch_shapes=[pltpu.VMEM((KV_WIDTH_A, tq + 2 * blk), BF16),
                        pltpu.VMEM((tq + 2 * blk, KV_WIDTH_A), BF16)],
        compiler_params=_params("parallel", "parallel"),
        name="window_attention",
    )(sink, tok, kt, kt, kt, tok, tok, tok, tok, kmt, vm, bias)


NB_ROWS_PER_TILE = 4
NB_TQ = NB_ROWS_PER_TILE * GRID_W
NB_TILES_PER_STEP = 2


def _nbr_bias_table(rpb):
    h, n_dr, n_dc = rpb.shape
    w = GRID_W
    lo = (w - 1) - (NA_COLS - 1)
    e = jnp.pad(rpb.astype(F32), ((0, 0), (0, 0), (lo, 2 * w - lo - n_dc)))
    toe = jnp.tile(e, (1, 1, w))[..., :w * (2 * w - 1)].reshape(h, n_dr, w, 2 * w - 1)
    blocks = toe[..., w - 1:]
    zero = jnp.zeros((h, w, w), F32)
    rows = []
    for a in range(NB_ROWS_PER_TILE):
        strip = []
        for b in range(3 * NB_ROWS_PER_TILE):
            dr = (b - NB_ROWS_PER_TILE) - a + (NA_ROWS - 1)
            strip.append(blocks[:, dr] if 0 <= dr < n_dr else zero)
        rows.append(jnp.concatenate(strip, axis=-1))
    return jnp.concatenate(rows, axis=1)


def _nbr_mask_table(n_tiles, n_rows):
    qi = np.arange(NB_TQ)[:, None]
    ki = np.arange(3 * NB_TQ)[None, :]
    c0 = np.clip(qi % GRID_W - NA_COLS // 2, 0, GRID_W - NA_COLS)
    k_col = ki % GRID_W
    col_ok = (k_col >= c0) & (k_col < c0 + NA_COLS)
    out = []
    for i in (0, min(1, n_tiles - 1), n_tiles - 1):
        q_row = i * NB_ROWS_PER_TILE + qi // GRID_W
        k_row = (i - 1) * NB_ROWS_PER_TILE + ki // GRID_W
        r0 = np.clip(q_row - NA_ROWS // 2, 0, n_rows - NA_ROWS)
        row_ok = (k_row >= r0) & (k_row < r0 + NA_ROWS)
        out.append(np.where(row_ok & col_ok, np.float32(0.0), np.float32(NEG)))
    return np.stack(out).astype(np.float32)


_NBR_CANONICAL_TILES = 3


def _nbr_logit_table(rpb):
    mask = _nbr_mask_table(_NBR_CANONICAL_TILES, _NBR_CANONICAL_TILES * NB_ROWS_PER_TILE)
    table = (_nbr_bias_table(rpb)[None] + jnp.asarray(mask)[:, None]) * LOG2E
    return table.reshape(3 * N_Q_HEADS, NB_TQ, 3 * NB_TQ)


def _nbr_kernel(q_ref, ktp_ref, ktc_ref, ktn_ref, vp_ref, vc_ref, vn_ref, qm_ref, kmt_ref, vm_ref, table_ref,
                o_ref, *, n_tiles):
    i = pl.program_id(1)
    tq = NB_TQ
    per = NB_TILES_PER_STEP
    low = _low_lanes(tq)
    kt_tiles = ([lambda cols: ktp_ref[0, cols, :]]
                + [(lambda cols, s=s: ktc_ref[0, cols, s * tq:(s + 1) * tq]) for s in range(per)]
                + [lambda cols: ktn_ref[0, cols, :]])
    v_tiles = ([lambda cols: vp_ref[0, :, cols]]
               + [(lambda cols, s=s: vc_ref[0, s * tq:(s + 1) * tq, cols]) for s in range(per)]
               + [lambda cols: vn_ref[0, :, cols]])
    for s in range(per):
        tile = i * per + s
        variant = jnp.where(tile == 0, 0, jnp.where(tile == n_tiles - 1, 2, 1))
        rows = slice(s * tq, (s + 1) * tq)
        for p in range(Q_WIDTH // LANES):
            cols = slice(p * LANES, (p + 1) * LANES)
            qs = jnp.concatenate(_split_heads(q_ref[0, rows, cols], low), axis=0)
            kt = jnp.concatenate([f(cols) for f in kt_tiles[s:s + 3]], axis=1)
            sc = jnp.dot(qs, kt, preferred_element_type=F32)
            sc = sc + jnp.concatenate([table_ref[variant * N_Q_HEADS + 2 * p],
                                       table_ref[variant * N_Q_HEADS + 2 * p + 1]], axis=0)
            pr = jnp.exp2(sc - jnp.max(sc, axis=-1, keepdims=True)).astype(BF16)
            v = jnp.concatenate([f(cols) for f in v_tiles[s:s + 3]], axis=0)
            oa = jnp.dot(pr, _with_ones(v), preferred_element_type=F32)
            o_ref[0, rows, cols] = _merge_heads(oa, low, slice(0, tq), slice(tq, 2 * tq)).astype(o_ref.dtype)

    _memory_attention(qm_ref, kmt_ref, vm_ref, o_ref)


def _nbr_attention(tok, kt, kmt, vm, table):
    b, t, _ = tok.shape
    tq = NB_TQ
    n = t // tq
    n_rows = t // GRID_W
    per = NB_TILES_PER_STEP
    ts = per * tq
    assert n_rows >= NA_ROWS and t % ts == 0 and n >= 2
    canonical = _nbr_mask_table(_NBR_CANONICAL_TILES, _NBR_CANONICAL_TILES * NB_ROWS_PER_TILE)
    assert n == 2 or np.array_equal(_nbr_mask_table(n, n_rows), canonical)
    assert n > 2 or np.array_equal(_nbr_mask_table(n, n_rows)[[0, 2]], canonical[[0, 2]])
    prev_i = lambda i: jnp.maximum(i * per - 1, 0)
    next_i = lambda i: jnp.minimum((i + 1) * per, n - 1)
    return pl.pallas_call(
        functools.partial(_nbr_kernel, n_tiles=n),
        grid=(b, t // ts),
        in_specs=[
            pl.BlockSpec((1, ts, Q_WIDTH), lambda bi, i: (bi, i, 0)),
            pl.BlockSpec((1, Q_WIDTH, tq), lambda bi, i: (bi, 0, prev_i(i))),
            pl.BlockSpec((1, Q_WIDTH, ts), lambda bi, i: (bi, 0, i)),
            pl.BlockSpec((1, Q_WIDTH, tq), lambda bi, i: (bi, 0, next_i(i))),
            pl.BlockSpec((1, tq, Q_WIDTH), lambda bi, i: (bi, prev_i(i), 1)),
            pl.BlockSpec((1, ts, Q_WIDTH), lambda bi, i: (bi, i, 1)),
            pl.BlockSpec((1, tq, Q_WIDTH), lambda bi, i: (bi, next_i(i), 1)),
            pl.BlockSpec((1, ts, MEM_WIDTH), lambda bi, i: (bi, i, 2 * Q_WIDTH // MEM_WIDTH)),
            pl.BlockSpec((1, MEM_WIDTH, MEM_LEN), lambda bi, i: (bi, 0, 0)),
            pl.BlockSpec((1, MEM_LEN, MEM_WIDTH), lambda bi, i: (bi, 0, 0)),
            _resident(table.shape, lambda bi, i: (0, 0, 0)),
        ],
        out_specs=pl.BlockSpec((1, ts, MIX_WIDTH), lambda bi, i: (bi, i, 0)),
        out_shape=jax.ShapeDtypeStruct((b, t, MIX_WIDTH), BF16),
        compiler_params=_params("parallel", "parallel"),
        name="nbr_attention",
    )(tok, kt, kt, kt, tok, tok, tok, tok, kmt, vm, table)


def _post_kernel(xp_ref, xc_ref, xn_ref, op_ref, oc_ref, on_ref, wo_ref, gn_ref, wg_ref, wu_ref, cw_ref, cb_ref,
                 wd_ref, gf_ref, y_ref, oext_ref, hext_ref, act_ref, *, final_norm):
    i = pl.program_id(1)
    n = pl.num_programs(1)
    tm = xc_ref.shape[1]
    ext = tm + HALO
    halo_row = lax.broadcasted_iota(jnp.int32, (HALO, 1), 0)
    take_next = halo_row < HALO // 2
    oext_ref[0:tm] = oc_ref[0]
    oext_ref[tm:] = jnp.where(take_next, on_ref[0], op_ref[0])
    mix = jnp.dot(oext_ref[...], wo_ref[...], preferred_element_type=F32)
    gn = gn_ref[...]
    x1 = xc_ref[0] + mix[0:tm]
    y_ref[0] = x1
    hext_ref[0:tm] = _rms(x1, gn).astype(BF16)
    h_halo = _rms(jnp.where(take_next, xn_ref[0], xp_ref[0]) + mix[tm:], gn).astype(BF16)
    halo_ok = ((halo_row == 0) & (i < n - 1)) | ((halo_row == HALO - 1) & (i > 0))
    hext_ref[tm:] = jnp.where(halo_ok, h_halo, jnp.zeros_like(h_halo))

    for c in range(D_FF // FF_CHUNK):
        cols = slice(c * FF_CHUNK, (c + 1) * FF_CHUNK)
        g_ext = jnp.dot(hext_ref[...], wg_ref[:, cols], preferred_element_type=F32)
        g = g_ext[0:tm]
        g_prev = pltpu.roll(g_ext, 1, 0)[0:tm]
        g_next = pltpu.roll(g_ext, ext - 1, 0)[0:tm]
        gc = g_prev * cw_ref[0:1, cols] + g * cw_ref[1:2, cols] + g_next * cw_ref[2:3, cols] + cb_ref[:, cols]
        u = jnp.dot(hext_ref[0:tm], wu_ref[:, cols], preferred_element_type=F32)
        act_ref[:, cols] = (gc * jax.nn.sigmoid(gc) * u).astype(BF16)

    y = y_ref[0] + jnp.dot(act_ref[...], wd_ref[...], preferred_element_type=F32)
    if final_norm:
        y = _rms(y, gf_ref[...])
    y_ref[0] = y


def _post_attention(x, o, w_o, g_ffn, w_gate, w_up, conv_w, conv_b, w_down, g_final, tm, final_norm):
    b, t, d = x.shape
    per = tm // HALO
    n_halo = t // HALO
    kernel = functools.partial(_post_kernel, final_norm=final_norm)
    const2 = lambda bi, i: (0, 0)
    prev_map = lambda bi, i: (bi, jnp.maximum(i * per - 1, 0), 0)
    cur_map = lambda bi, i: (bi, i, 0)
    next_map = lambda bi, i: (bi, jnp.minimum((i + 1) * per, n_halo - 1), 0)
    return pl.pallas_call(
        kernel,
        grid=(b, t // tm),
        in_specs=[
            pl.BlockSpec((1, HALO, d), prev_map),
            pl.BlockSpec((1, tm, d), cur_map),
            pl.BlockSpec((1, HALO, d), next_map),
            pl.BlockSpec((1, HALO, MIX_WIDTH), prev_map),
            pl.BlockSpec((1, tm, MIX_WIDTH), cur_map),
            pl.BlockSpec((1, HALO, MIX_WIDTH), next_map),
            _resident(w_o.shape, const2),
            _resident((1, d), const2),
            _resident(w_gate.shape, const2),
            _resident(w_up.shape, const2),
            _resident(conv_w.shape, const2),
            _resident(conv_b.shape, const2),
            _resident(w_down.shape, const2),
            _resident((1, d), const2),
        ],
        out_specs=pl.BlockSpec((1, tm, d), cur_map),
        out_shape=jax.ShapeDtypeStruct((b, t, d), F32),
        scratch_shapes=[pltpu.VMEM((tm + HALO, MIX_WIDTH), BF16), pltpu.VMEM((tm + HALO, d), BF16),
                        pltpu.VMEM((tm, D_FF), BF16)],
        compiler_params=_params("parallel", "parallel"),
        name="post_attention",
    )(x, x, x, o, o, o, w_o, g_ffn.reshape(1, d), w_gate, w_up, conv_w, conv_b, w_down, g_final.reshape(1, d))


def _prepare_weights(g_mix, g_mem, w_in_a, sink_a, w_in_b, rpb_b, w_mem_kv, w_o, g_ffn, w_gate, w_up, conv_w,
                     conv_b, w_down, g_final):
    head_cols = np.arange(Q_WIDTH).reshape(N_Q_HEADS, HEAD_DIM)
    perm_a = head_cols[_window_head_order()].reshape(-1)
    layers = []
    for i in range(DEPTH):
        j = i // 2
        wo = w_o[i]
        if i % 2 == 0:
            w = w_in_a[j]
            wq, wk, wv, wqm = (w[:, :Q_WIDTH], w[:, Q_WIDTH:Q_WIDTH + KV_WIDTH_A],
                               w[:, Q_WIDTH + KV_WIDTH_A:Q_WIDTH + 2 * KV_WIDTH_A], w[:, Q_WIDTH + 2 * KV_WIDTH_A:])
            wq = wq[:, perm_a]
            wo = jnp.concatenate([wo[:Q_WIDTH][perm_a], wo[Q_WIDTH:]], axis=0)
            extra = dict(sink=sink_a[j].astype(F32) * LOG2E, bias=jnp.asarray(_window_bias_table()))
        else:
            w = w_in_b[j]
            wq, wk, wv, wqm = (w[:, :Q_WIDTH], w[:, Q_WIDTH:2 * Q_WIDTH], w[:, 2 * Q_WIDTH:3 * Q_WIDTH],
                               w[:, 3 * Q_WIDTH:])
            extra = dict(bias=_nbr_logit_table(rpb_b[j]))
        tok_scale = np.concatenate([np.full(wq.shape[1], Q_SCALE), np.ones(wv.shape[1]),
                                    np.full(wqm.shape[1], Q_SCALE)]).astype(np.float32)
        layers.append(dict(
            g_mix=g_mix[i], g_mem=g_mem[i], g_ffn=g_ffn[i],
            w_tok=jnp.concatenate([wq, wv, wqm], axis=1).astype(BF16), tok_scale=jnp.asarray(tok_scale),
            w_kt=wk.T.astype(BF16),
            w_vm=w_mem_kv[i][:, MEM_WIDTH:].astype(BF16), w_kmt=w_mem_kv[i][:, :MEM_WIDTH].T.astype(BF16),
            w_o=wo.astype(BF16),
            w_gate=w_gate[i].astype(BF16), w_up=w_up[i].astype(BF16),
            conv_w=conv_w[i], conv_b=conv_b[i].reshape(1, D_FF),
            w_down=w_down[i].astype(BF16), **extra))
    return layers


def _trunk(x, mem, layers, g_final):
    b, t, d = x.shape
    n = b * t
    tm = 512
    mem2 = mem.reshape(b * MEM_LEN, d)
    ones_mem = jnp.ones((MEM_WIDTH,), F32)
    for i, lw in enumerate(layers):
        tok, kt = _norm_project(x.reshape(n, d), lw["g_mix"], lw["w_tok"], lw["tok_scale"], lw["w_kt"], b, tm)
        vm, kmt = _norm_project(mem2, lw["g_mem"], lw["w_vm"], ones_mem, lw["w_kmt"], b, MEM_LEN)
        tok = tok.reshape(b, t, -1)
        vm = vm.reshape(b, MEM_LEN, MEM_WIDTH)
        if i % 2 == 0:
            o = _window_attention(tok, kt, kmt, vm, lw["sink"], lw["bias"], tq=512)
        else:
            o = _nbr_attention(tok, kt, kmt, vm, lw["bias"])
        x = _post_attention(x, o, lw["w_o"], lw["g_ffn"], lw["w_gate"], lw["w_up"], lw["conv_w"], lw["conv_b"],
                            lw["w_down"], g_final, tm, final_norm=(i == DEPTH - 1))
    return x


def kernel(x_prompt, x_sample, mem_prompt, mem_sample, g_mix, g_mem, w_in_a, sink_a, w_in_b, rpb_b, w_mem_kv, w_o,
           g_ffn, w_gate, w_up, conv_w, conv_b, w_down, g_final):
    layers = _prepare_weights(g_mix, g_mem, w_in_a, sink_a, w_in_b, rpb_b, w_mem_kv, w_o, g_ffn, w_gate, w_up,
                              conv_w, conv_b, w_down, g_final)
    y_prompt = _trunk(x_prompt, mem_prompt, layers, g_final)
    y_sample = _trunk(x_sample, mem_sample, layers, g_final)
    return (y_prompt, y_sample)
```

```python
import functools

import numpy as np
import jax
import jax.numpy as jnp
from jax import lax
from jax.experimental import pallas as pl
from jax.experimental.pallas import tpu as pltpu

D_MODEL = 1024
DEPTH = 2
HEAD_DIM = 64
N_Q_HEADS = 12
N_KV_HEADS_A = 4
N_MEM_HEADS = 4
MEM_LEN = 256
WINDOW = 128
GRID_W = 64
NA_ROWS = 8
NA_COLS = 16
D_FF = 2816
CONV_W = 3
EPS = 1e-6
NEG = -1e30
Q_WIDTH = N_Q_HEADS * HEAD_DIM
KV_WIDTH_A = N_KV_HEADS_A * HEAD_DIM
MEM_WIDTH = N_MEM_HEADS * HEAD_DIM
MIX_WIDTH = Q_WIDTH + MEM_WIDTH
LOG2E = 1.4426950408889634
Q_SCALE = HEAD_DIM ** -0.5 * LOG2E
REP_A = N_Q_HEADS // N_KV_HEADS_A

F32 = jnp.float32
BF16 = jnp.bfloat16

LANES = 128
VMEM_LIMIT_BYTES = 56 * 1024 * 1024
FF_CHUNK = 256
HALO = 16

_NT = (((1,), (1,)), ((), ()))


def _params(*sem):
    return pltpu.CompilerParams(dimension_semantics=sem, vmem_limit_bytes=VMEM_LIMIT_BYTES)


def _resident(shape, index_map):
    return pl.BlockSpec(shape, index_map, pipeline_mode=pl.Buffered(1))


def _rms(x, g):
    return x * lax.rsqrt(jnp.mean(x * x, axis=-1, keepdims=True) + EPS) * g


def _proj_kernel(x_ref, g_ref, w_ref, cs_ref, wt_ref, tok_ref, feat_ref):
    half = x_ref.shape[0] // 2
    for r in range(2):
        rows = slice(r * half, (r + 1) * half)
        h = _rms(x_ref[rows, :], g_ref[...]).astype(BF16)
        tok = jnp.dot(h, w_ref[...], preferred_element_type=F32) * cs_ref[...]
        tok_ref[rows, :] = tok.astype(tok_ref.dtype)
        feat_ref[0, :, rows] = lax.dot_general(wt_ref[...], h, _NT, preferred_element_type=F32).astype(feat_ref.dtype)


def _norm_project(x, g, w_tok, col_scale, w_feat_t, batch, tm):
    n, d = x.shape
    t = n // batch
    per_seq = t // tm
    wdt = w_tok.shape[1]
    wft = w_feat_t.shape[0]
    return pl.pallas_call(
        _proj_kernel,
        grid=(n // tm,),
        in_specs=[
            pl.BlockSpec((tm, d), lambda i: (i, 0)),
            _resident((1, d), lambda i: (0, 0)),
            _resident((d, wdt), lambda i: (0, 0)),
            _resident((1, wdt), lambda i: (0, 0)),
            _resident((wft, d), lambda i: (0, 0)),
        ],
        out_specs=[pl.BlockSpec((tm, wdt), lambda i: (i, 0)),
                   pl.BlockSpec((1, wft, tm), lambda i: (i // per_seq, 0, i % per_seq))],
        out_shape=[jax.ShapeDtypeStruct((n, wdt), BF16), jax.ShapeDtypeStruct((batch, wft, t), BF16)],
        compiler_params=_params("parallel"),
        name="norm_project",
    )(x, g.reshape(1, d), w_tok, col_scale.reshape(1, wdt), w_feat_t)


def _mem_proj_kernel(m_ref, g_ref, wv_ref, wkt_ref, *out_refs):
    x = m_ref[...]
    y = x * lax.rsqrt(jnp.mean(x * x, axis=-1, keepdims=True) + EPS)
    n_seq = m_ref.shape[0] // MEM_LEN
    for l in range(DEPTH):
        vm_ref, kmt_ref = out_refs[2 * l], out_refs[2 * l + 1]
        h = (y * g_ref[l:l + 1, :]).astype(BF16)
        vm_ref[...] = jnp.dot(h, wv_ref[l], preferred_element_type=F32).astype(vm_ref.dtype)
        for s in range(n_seq):
            kmt_ref[s] = lax.dot_general(wkt_ref[l], h[s * MEM_LEN:(s + 1) * MEM_LEN], _NT,
                                         preferred_element_type=F32).astype(kmt_ref.dtype)


def _memory_project(mem, g_mem, w_vm, w_kmt):
    b, _, d = mem.shape
    n_seq = next(c for c in (4, 2, 1) if b % c == 0)
    tm = n_seq * MEM_LEN
    outs = pl.pallas_call(
        _mem_proj_kernel,
        grid=(b // n_seq,),
        in_specs=[
            pl.BlockSpec((tm, d), lambda i: (i, 0)),
            _resident(g_mem.shape, lambda i: (0, 0)),
            _resident(w_vm.shape, lambda i: (0, 0, 0)),
            _resident(w_kmt.shape, lambda i: (0, 0, 0)),
        ],
        out_specs=[pl.BlockSpec((tm, MEM_WIDTH), lambda i: (i, 0)),
                   pl.BlockSpec((n_seq, MEM_WIDTH, MEM_LEN), lambda i: (i, 0, 0))] * DEPTH,
        out_shape=[jax.ShapeDtypeStruct((b * MEM_LEN, MEM_WIDTH), BF16),
                   jax.ShapeDtypeStruct((b, MEM_WIDTH, MEM_LEN), BF16)] * DEPTH,
        compiler_params=_params("parallel"),
        name="memory_project",
    )(mem.reshape(b * MEM_LEN, d), g_mem, w_vm, w_kmt)
    return [(outs[2 * l].reshape(b, MEM_LEN, MEM_WIDTH), outs[2 * l + 1]) for l in range(DEPTH)]


def _low_lanes(rows):
    return lax.broadcasted_iota(jnp.int32, (rows, LANES), 1) < HEAD_DIM


def _split_heads(q_tile, low):
    zero = jnp.zeros_like(q_tile)
    return jnp.where(low, q_tile, zero), jnp.where(low, zero, q_tile)


def _with_ones(v):
    return jnp.concatenate([v, jnp.ones_like(v)], axis=1)


def _merge_heads(oa, low, lo_rows, hi_rows, extra_den=None):
    pick = lambda a: jnp.where(low, a[lo_rows], a[hi_rows])
    den = pick(oa[:, LANES:])
    if extra_den is not None:
        den = den + pick(extra_den)
    return pick(oa[:, :LANES]) / den


def _run_pipelined(stages):
    sc = stages[0][0]()
    for k, (_, finish) in enumerate(stages):
        sc_next = stages[k + 1][0]() if k + 1 < len(stages) else None
        finish(sc)
        sc = sc_next


def _memory_stages(qm_ref, kmt_ref, vm_ref, o_ref, tq):
    low = _low_lanes(tq)
    stages = []
    for r in range(qm_ref.shape[1] // tq):
        rows = slice(r * tq, (r + 1) * tq)
        for p in range(MEM_WIDTH // LANES):
            cols = slice(p * LANES, (p + 1) * LANES)

            def scores(rows=rows, cols=cols):
                qs = jnp.concatenate(_split_heads(qm_ref[0, rows, cols], low), axis=0)
                return jnp.dot(qs, kmt_ref[0, cols, :], preferred_element_type=F32)

            def finish(s, rows=rows, cols=cols, p=p):
                pr = jnp.exp2(s - jnp.max(s, axis=-1, keepdims=True)).astype(BF16)
                oa = jnp.dot(pr, _with_ones(vm_ref[0, :, cols]), preferred_element_type=F32)
                o_ref[0, rows, Q_WIDTH + p * LANES:Q_WIDTH + (p + 1) * LANES] = (
                    _merge_heads(oa, low, slice(0, tq), slice(tq, 2 * tq)).astype(o_ref.dtype))

            stages.append((scores, finish))
    return stages


def _window_head(pair, row_block):
    return REP_A * (2 * pair + row_block % 2) + row_block // 2


def _window_head_order():
    return [_window_head(p, rb) for p in range(N_KV_HEADS_A // 2) for rb in range(2 * REP_A)]


def _window_bias_table():
    slopes = (2.0 ** (-8.0 * np.arange(1, N_Q_HEADS + 1, dtype=np.float32) / N_Q_HEADS)).astype(np.float32)
    blk = WINDOW
    qq = np.arange(blk)[:, None]
    kk = np.arange(3 * blk)[None, :]
    arel = np.abs(qq - kk + WINDOW)
    in_window = arel <= WINDOW
    in_seq = [kk >= blk, np.ones_like(kk, bool), kk < 2 * blk]
    out = np.empty((3, N_KV_HEADS_A // 2, 2 * REP_A, blk, 3 * blk), np.float32)
    for var in range(3):
        for p in range(N_KV_HEADS_A // 2):
            for rb in range(2 * REP_A):
                alibi = -slopes[_window_head(p, rb)] * arel.astype(np.float32) * np.float32(LOG2E)
                out[var, p, rb] = np.where(in_window & in_seq[var], alibi, np.float32(NEG))
    return out.reshape(3 * (N_KV_HEADS_A // 2), 2 * REP_A * blk, 3 * blk)


def _window_kernel(sink_ref, q_ref, ktp_ref, ktc_ref, ktn_ref, vp_ref, vc_ref, vn_ref, qm_ref, kmt_ref, vm_ref,
                   bias_ref, o_ref, ktbuf, vbuf, *, tq, n_blk):
    i = pl.program_id(1)
    blk = WINDOW
    n_pair = N_KV_HEADS_A // 2
    ktbuf[:, 0:blk] = ktp_ref[0]
    ktbuf[:, blk:blk + tq] = ktc_ref[0]
    ktbuf[:, blk + tq:] = ktn_ref[0]
    vbuf[0:blk] = vp_ref[0]
    vbuf[blk:blk + tq] = vc_ref[0]
    vbuf[blk + tq:] = vn_ref[0]

    low = _low_lanes(blk)
    sinks = [jnp.concatenate([jnp.full((blk, LANES), sink_ref[_window_head(p, rb)], F32)
                              for rb in range(2 * REP_A)], axis=0) for p in range(n_pair)]
    def scores(j, p, t):
        g_blk = i * (tq // blk) + j
        variant = jnp.where(g_blk == 0, 0, jnp.where(g_blk == n_blk - 1, 2, 1))
        c0 = (REP_A * p + t) * LANES
        qs = jnp.concatenate(_split_heads(q_ref[0, j * blk:(j + 1) * blk, c0:c0 + LANES], low), axis=0)
        sc = jnp.dot(qs, ktbuf[p * LANES:(p + 1) * LANES, j * blk:(j + 3) * blk], preferred_element_type=F32)
        return sc + bias_ref[variant * n_pair + p, 2 * t * blk:(2 * t + 2) * blk, :]

    def finish(j, p, t, sc):
        sink = sinks[p][2 * t * blk:(2 * t + 2) * blk]
        m = jnp.maximum(jnp.max(sc, axis=-1, keepdims=True), sink)
        pr = jnp.concatenate([jnp.exp2(sc[:, c * LANES:(c + 1) * LANES] - m) for c in range(3 * blk // LANES)],
                             axis=1).astype(BF16)
        oa = jnp.dot(pr, _with_ones(vbuf[j * blk:(j + 3) * blk, p * LANES:(p + 1) * LANES]),
                     preferred_element_type=F32)
        c0 = (REP_A * p + t) * LANES
        o_ref[0, j * blk:(j + 1) * blk, c0:c0 + LANES] = _merge_heads(
            oa, low, slice(0, blk), slice(blk, 2 * blk), extra_den=jnp.exp2(sink - m)).astype(o_ref.dtype)

    stages = [(functools.partial(scores, j, p, t), functools.partial(finish, j, p, t))
              for j in range(tq // blk) for p in range(n_pair) for t in range(REP_A)]
    _run_pipelined(stages + _memory_stages(qm_ref, kmt_ref, vm_ref, o_ref, 2 * blk))


def _window_attention(tok, kt, kmt, vm, sink, bias, tq):
    b, t, _ = tok.shape
    blk = WINDOW
    n_blk = t // blk
    per = tq // blk
    assert n_blk >= 2 and t % tq == 0
    vcol = Q_WIDTH // KV_WIDTH_A
    prev_i = lambda i: jnp.maximum(i * per - 1, 0)
    next_i = lambda i: jnp.minimum((i + 1) * per, n_blk - 1)
    kernel = functools.partial(_window_kernel, tq=tq, n_blk=n_blk)
    return pl.pallas_call(
        kernel,
        grid=(b, t // tq),
        in_specs=[
            pl.BlockSpec(memory_space=pltpu.SMEM),
            pl.BlockSpec((1, tq, Q_WIDTH), lambda bi, i: (bi, i, 0)),
            pl.BlockSpec((1, KV_WIDTH_A, blk), lambda bi, i: (bi, 0, prev_i(i))),
            pl.BlockSpec((1, KV_WIDTH_A, tq), lambda bi, i: (bi, 0, i)),
            pl.BlockSpec((1, KV_WIDTH_A, blk), lambda bi, i: (bi, 0, next_i(i))),
            pl.BlockSpec((1, blk, KV_WIDTH_A), lambda bi, i: (bi, prev_i(i), vcol)),
            pl.BlockSpec((1, tq, KV_WIDTH_A), lambda bi, i: (bi, i, vcol)),
            pl.BlockSpec((1, blk, KV_WIDTH_A), lambda bi, i: (bi, next_i(i), vcol)),
            pl.BlockSpec((1, tq, MEM_WIDTH), lambda bi, i: (bi, i, vcol + 1)),
            pl.BlockSpec((1, MEM_WIDTH, MEM_LEN), lambda bi, i: (bi, 0, 0)),
            pl.BlockSpec((1, MEM_LEN, MEM_WIDTH), lambda bi, i: (bi, 0, 0)),
            _resident(bias.shape, lambda bi, i: (0, 0, 0)),
        ],
        out_specs=pl.BlockSpec((1, tq, MIX_WIDTH), lambda bi, i: (bi, i, 0)),
        out_shape=jax.ShapeDtypeStruct((b, t, MIX_WIDTH), BF16),
        scratch_shapes=[pltpu.VMEM((KV_WIDTH_A, tq + 2 * blk), BF16),
                        pltpu.VMEM((tq + 2 * blk, KV_WIDTH_A), BF16)],
        compiler_params=_params("parallel", "parallel"),
        name="window_attention",
    )(sink, tok, kt, kt, kt, tok, tok, tok, tok, kmt, vm, bias)


NB_ROWS_PER_TILE = 4
NB_TQ = NB_ROWS_PER_TILE * GRID_W
NB_TILES_PER_STEP = 2


def _nbr_bias_table(rpb):
    h, n_dr, n_dc = rpb.shape
    w = GRID_W
    lo = (w - 1) - (NA_COLS - 1)
    e = jnp.pad(rpb.astype(F32), ((0, 0), (0, 0), (lo, 2 * w - lo - n_dc)))
    toe = jnp.tile(e, (1, 1, w))[..., :w * (2 * w - 1)].reshape(h, n_dr, w, 2 * w - 1)
    blocks = toe[..., w - 1:]
    zero = jnp.zeros((h, w, w), F32)
    rows = []
    for a in range(NB_ROWS_PER_TILE):
        strip = []
        for b in range(3 * NB_ROWS_PER_TILE):
            dr = (b - NB_ROWS_PER_TILE) - a + (NA_ROWS - 1)
            strip.append(blocks[:, dr] if 0 <= dr < n_dr else zero)
        rows.append(jnp.concatenate(strip, axis=-1))
    return jnp.concatenate(rows, axis=1)


def _nbr_mask_table(n_tiles, n_rows):
    qi = np.arange(NB_TQ)[:, None]
    ki = np.arange(3 * NB_TQ)[None, :]
    c0 = np.clip(qi % GRID_W - NA_COLS // 2, 0, GRID_W - NA_COLS)
    k_col = ki % GRID_W
    col_ok = (k_col >= c0) & (k_col < c0 + NA_COLS)
    out = []
    for i in (0, min(1, n_tiles - 1), n_tiles - 1):
        q_row = i * NB_ROWS_PER_TILE + qi // GRID_W
        k_row = (i - 1) * NB_ROWS_PER_TILE + ki // GRID_W
        r0 = np.clip(q_row - NA_ROWS // 2, 0, n_rows - NA_ROWS)
        row_ok = (k_row >= r0) & (k_row < r0 + NA_ROWS)
        out.append(np.where(row_ok & col_ok, np.float32(0.0), np.float32(NEG)))
    return np.stack(out).astype(np.float32)


_NBR_CANONICAL_TILES = 3


def _nbr_logit_table(rpb):
    mask = _nbr_mask_table(_NBR_CANONICAL_TILES, _NBR_CANONICAL_TILES * NB_ROWS_PER_TILE)
    table = (_nbr_bias_table(rpb)[None] + jnp.asarray(mask)[:, None]) * LOG2E
    return table.reshape(3 * N_Q_HEADS, NB_TQ, 3 * NB_TQ)


def _nbr_kernel(q_ref, ktp_ref, ktc_ref, ktn_ref, vp_ref, vc_ref, vn_ref, qm_ref, kmt_ref, vm_ref, table_ref,
                o_ref, *, n_tiles):
    i = pl.program_id(1)
    tq = NB_TQ
    per = NB_TILES_PER_STEP
    low = _low_lanes(tq)
    kt_tiles = ([lambda cols: ktp_ref[0, cols, :]]
                + [(lambda cols, s=s: ktc_ref[0, cols, s * tq:(s + 1) * tq]) for s in range(per)]
                + [lambda cols: ktn_ref[0, cols, :]])
    v_tiles = ([lambda cols: vp_ref[0, :, cols]]
               + [(lambda cols, s=s: vc_ref[0, s * tq:(s + 1) * tq, cols]) for s in range(per)]
               + [lambda cols: vn_ref[0, :, cols]])
    def scores(s, p, hi):
        tile = i * per + s
        variant = jnp.where(tile == 0, 0, jnp.where(tile == n_tiles - 1, 2, 1))
        cols = slice(p * LANES, (p + 1) * LANES)
        qs = _split_heads(q_ref[0, s * tq:(s + 1) * tq, cols], low)[hi]
        kt = jnp.concatenate([f(cols) for f in kt_tiles[s:s + 3]], axis=1)
        sc = jnp.dot(qs, kt, preferred_element_type=F32)
        return sc + table_ref[variant * N_Q_HEADS + 2 * p + hi]

    pending = {}

    def finish(s, p, hi, sc):
        cols = slice(p * LANES, (p + 1) * LANES)
        pr = jnp.exp2(sc - jnp.max(sc, axis=-1, keepdims=True)).astype(BF16)
        v = jnp.concatenate([f(cols) for f in v_tiles[s:s + 3]], axis=0)
        oa = jnp.dot(pr, _with_ones(v), preferred_element_type=F32)
        if not hi:
            pending[(s, p)] = oa
            return
        both = jnp.concatenate([pending.pop((s, p)), oa], axis=0)
        o_ref[0, s * tq:(s + 1) * tq, cols] = (
            _merge_heads(both, low, slice(0, tq), slice(tq, 2 * tq)).astype(o_ref.dtype))

    stages = [(functools.partial(scores, s, p, hi), functools.partial(finish, s, p, hi))
              for s in range(per) for p in range(Q_WIDTH // LANES) for hi in range(2)]
    _run_pipelined(stages + _memory_stages(qm_ref, kmt_ref, vm_ref, o_ref, tq))


def _nbr_attention(tok, kt, kmt, vm, table):
    b, t, _ = tok.shape
    tq = NB_TQ
    n = t // tq
    n_rows = t // GRID_W
    per = NB_TILES_PER_STEP
    ts = per * tq
    assert n_rows >= NA_ROWS and t % ts == 0 and n >= 2
    canonical = _nbr_mask_table(_NBR_CANONICAL_TILES, _NBR_CANONICAL_TILES * NB_ROWS_PER_TILE)
    assert n == 2 or np.array_equal(_nbr_mask_table(n, n_rows), canonical)
    assert n > 2 or np.array_equal(_nbr_mask_table(n, n_rows)[[0, 2]], canonical[[0, 2]])
    prev_i = lambda i: jnp.maximum(i * per - 1, 0)
    next_i = lambda i: jnp.minimum((i + 1) * per, n - 1)
    return pl.pallas_call(
        functools.partial(_nbr_kernel, n_tiles=n),
        grid=(b, t // ts),
        in_specs=[
            pl.BlockSpec((1, ts, Q_WIDTH), lambda bi, i: (bi, i, 0)),
            pl.BlockSpec((1, Q_WIDTH, tq), lambda bi, i: (bi, 0, prev_i(i))),
            pl.BlockSpec((1, Q_WIDTH, ts), lambda bi, i: (bi, 0, i)),
            pl.BlockSpec((1, Q_WIDTH, tq), lambda bi, i: (bi, 0, next_i(i))),
            pl.BlockSpec((1, tq, Q_WIDTH), lambda bi, i: (bi, prev_i(i), 1)),
            pl.BlockSpec((1, ts, Q_WIDTH), lambda bi, i: (bi, i, 1)),
            pl.BlockSpec((1, tq, Q_WIDTH), lambda bi, i: (bi, next_i(i), 1)),
            pl.BlockSpec((1, ts, MEM_WIDTH), lambda bi, i: (bi, i, 2 * Q_WIDTH // MEM_WIDTH)),
            pl.BlockSpec((1, MEM_WIDTH, MEM_LEN), lambda bi, i: (bi, 0, 0)),
            pl.BlockSpec((1, MEM_LEN, MEM_WIDTH), lambda bi, i: (bi, 0, 0)),
            _resident(table.shape, lambda bi, i: (0, 0, 0)),
        ],
        out_specs=pl.BlockSpec((1, ts, MIX_WIDTH), lambda bi, i: (bi, i, 0)),
        out_shape=jax.ShapeDtypeStruct((b, t, MIX_WIDTH), BF16),
        compiler_params=_params("parallel", "parallel"),
        name="nbr_attention",
    )(tok, kt, kt, kt, tok, tok, tok, tok, kmt, vm, table)


def _post_kernel(xp_ref, xc_ref, xn_ref, op_ref, oc_ref, on_ref, wo_ref, gn_ref, wg_ref, wu_ref, cw_ref, cb_ref,
                 wd_ref, gf_ref, y_ref, oext_ref, hext_ref, act_ref, *, final_norm):
    i = pl.program_id(1)
    n = pl.num_programs(1)
    tm = xc_ref.shape[1]
    ext = tm + HALO
    halo_row = lax.broadcasted_iota(jnp.int32, (HALO, 1), 0)
    take_next = halo_row < HALO // 2
    oext_ref[0:tm] = oc_ref[0]
    oext_ref[tm:] = jnp.where(take_next, on_ref[0], op_ref[0])
    mix = jnp.dot(oext_ref[...], wo_ref[...], preferred_element_type=F32)
    gn = gn_ref[...]
    x1 = xc_ref[0] + mix[0:tm]
    y_ref[0] = x1
    hext_ref[0:tm] = _rms(x1, gn).astype(BF16)
    h_halo = _rms(jnp.where(take_next, xn_ref[0], xp_ref[0]) + mix[tm:], gn).astype(BF16)
    halo_ok = ((halo_row == 0) & (i < n - 1)) | ((halo_row == HALO - 1) & (i > 0))
    hext_ref[tm:] = jnp.where(halo_ok, h_halo, jnp.zeros_like(h_halo))

    for c in range(D_FF // FF_CHUNK):
        cols = slice(c * FF_CHUNK, (c + 1) * FF_CHUNK)
        g_ext = jnp.dot(hext_ref[...], wg_ref[:, cols], preferred_element_type=F32)
        g = g_ext[0:tm]
        g_prev = pltpu.roll(g_ext, 1, 0)[0:tm]
        g_next = pltpu.roll(g_ext, ext - 1, 0)[0:tm]
        gc = g_prev * cw_ref[0:1, cols] + g * cw_ref[1:2, cols] + g_next * cw_ref[2:3, cols] + cb_ref[:, cols]
        u = jnp.dot(hext_ref[0:tm], wu_ref[:, cols], preferred_element_type=F32)
        act_ref[:, cols] = (gc * jax.nn.sigmoid(gc) * u).astype(BF16)

    y = y_ref[0] + jnp.dot(act_ref[...], wd_ref[...], preferred_element_type=F32)
    if final_norm:
        y = _rms(y, gf_ref[...])
    y_ref[0] = y


def _post_attention(x, o, w_o, g_ffn, w_gate, w_up, conv_w, conv_b, w_down, g_final, tm, final_norm):
    b, t, d = x.shape
    per = tm // HALO
    n_halo = t // HALO
    kernel = functools.partial(_post_kernel, final_norm=final_norm)
    const2 = lambda bi, i: (0, 0)
    prev_map = lambda bi, i: (bi, jnp.maximum(i * per - 1, 0), 0)
    cur_map = lambda bi, i: (bi, i, 0)
    next_map = lambda bi, i: (bi, jnp.minimum((i + 1) * per, n_halo - 1), 0)
    return pl.pallas_call(
        kernel,
        grid=(b, t // tm),
        in_specs=[
            pl.BlockSpec((1, HALO, d), prev_map),
            pl.BlockSpec((1, tm, d), cur_map),
            pl.BlockSpec((1, HALO, d), next_map),
            pl.BlockSpec((1, HALO, MIX_WIDTH), prev_map),
            pl.BlockSpec((1, tm, MIX_WIDTH), cur_map),
            pl.BlockSpec((1, HALO, MIX_WIDTH), next_map),
            _resident(w_o.shape, const2),
            _resident((1, d), const2),
            _resident(w_gate.shape, const2),
            _resident(w_up.shape, const2),
            _resident(conv_w.shape, const2),
            _resident(conv_b.shape, const2),
            _resident(w_down.shape, const2),
            _resident((1, d), const2),
        ],
        out_specs=pl.BlockSpec((1, tm, d), cur_map),
        out_shape=jax.ShapeDtypeStruct((b, t, d), F32),
        scratch_shapes=[pltpu.VMEM((tm + HALO, MIX_WIDTH), BF16), pltpu.VMEM((tm + HALO, d), BF16),
                        pltpu.VMEM((tm, D_FF), BF16)],
        compiler_params=_params("parallel", "parallel"),
        name="post_attention",
    )(x, x, x, o, o, o, w_o, g_ffn.reshape(1, d), w_gate, w_up, conv_w, conv_b, w_down, g_final.reshape(1, d))


def _prepare_weights(g_mix, g_mem, w_in_a, sink_a, w_in_b, rpb_b, w_mem_kv, w_o, g_ffn, w_gate, w_up, conv_w,
                     conv_b, w_down, g_final):
    head_cols = np.arange(Q_WIDTH).reshape(N_Q_HEADS, HEAD_DIM)
    perm_a = head_cols[_window_head_order()].reshape(-1)
    layers = []
    for i in range(DEPTH):
        j = i // 2
        wo = w_o[i]
        if i % 2 == 0:
            w = w_in_a[j]
            wq, wk, wv, wqm = (w[:, :Q_WIDTH], w[:, Q_WIDTH:Q_WIDTH + KV_WIDTH_A],
                               w[:, Q_WIDTH + KV_WIDTH_A:Q_WIDTH + 2 * KV_WIDTH_A], w[:, Q_WIDTH + 2 * KV_WIDTH_A:])
            wq = wq[:, perm_a]
            wo = jnp.concatenate([wo[:Q_WIDTH][perm_a], wo[Q_WIDTH:]], axis=0)
            extra = dict(sink=sink_a[j].astype(F32) * LOG2E, bias=jnp.asarray(_window_bias_table()))
        else:
            w = w_in_b[j]
            wq, wk, wv, wqm = (w[:, :Q_WIDTH], w[:, Q_WIDTH:2 * Q_WIDTH], w[:, 2 * Q_WIDTH:3 * Q_WIDTH],
                               w[:, 3 * Q_WIDTH:])
            extra = dict(bias=_nbr_logit_table(rpb_b[j]))
        tok_scale = np.concatenate([np.full(wq.shape[1], Q_SCALE), np.ones(wv.shape[1]),
                                    np.full(wqm.shape[1], Q_SCALE)]).astype(np.float32)
        layers.append(dict(
            g_mix=g_mix[i], g_ffn=g_ffn[i],
            w_tok=jnp.concatenate([wq, wv, wqm], axis=1).astype(BF16), tok_scale=jnp.asarray(tok_scale),
            w_kt=wk.T.astype(BF16),
            w_o=wo.astype(BF16),
            w_gate=w_gate[i].astype(BF16), w_up=w_up[i].astype(BF16),
            conv_w=conv_w[i], conv_b=conv_b[i].reshape(1, D_FF),
            w_down=w_down[i].astype(BF16), **extra))
    mem_weights = dict(g_mem=g_mem, w_vm=w_mem_kv[:, :, MEM_WIDTH:].astype(BF16),
                       w_kmt=jnp.swapaxes(w_mem_kv[:, :, :MEM_WIDTH], 1, 2).astype(BF16))
    return layers, mem_weights


def _trunk(x, mem, layers, mem_weights, g_final):
    b, t, d = x.shape
    n = b * t
    tm = 512
    mem_kv = _memory_project(mem, mem_weights["g_mem"], mem_weights["w_vm"], mem_weights["w_kmt"])
    for i, lw in enumerate(layers):
        tok, kt = _norm_project(x.reshape(n, d), lw["g_mix"], lw["w_tok"], lw["tok_scale"], lw["w_kt"], b, 2 * tm)
        tok = tok.reshape(b, t, -1)
        vm, kmt = mem_kv[i]
        if i % 2 == 0:
            o = _window_attention(tok, kt, kmt, vm, lw["sink"], lw["bias"], tq=512)
        else:
            o = _nbr_attention(tok, kt, kmt, vm, lw["bias"])
        x = _post_attention(x, o, lw["w_o"], lw["g_ffn"], lw["w_gate"], lw["w_up"], lw["conv_w"], lw["conv_b"],
                            lw["w_down"], g_final, tm, final_norm=(i == DEPTH - 1))
    return x


def kernel(x_prompt, x_sample, mem_prompt, mem_sample, g_mix, g_mem, w_in_a, sink_a, w_in_b, rpb_b, w_mem_kv, w_o,
           g_ffn, w_gate, w_up, conv_w, conv_b, w_down, g_final):
    layers, mem_weights = _prepare_weights(g_mix, g_mem, w_in_a, sink_a, w_in_b, rpb_b, w_mem_kv, w_o, g_ffn, w_gate,
                                           w_up, conv_w, conv_b, w_down, g_final)
    y_prompt = _trunk(x_prompt, mem_prompt, layers, mem_weights, g_final)
    y_sample = _trunk(x_sample, mem_sample, layers, mem_weights, g_final)
    return (y_prompt, y_sample)
```

```python
import functools

import numpy as np
import jax
import jax.numpy as jnp
from jax import lax
from jax.experimental import pallas as pl
from jax.experimental.pallas import tpu as pltpu

D_MODEL = 1024
DEPTH = 2
HEAD_DIM = 64
N_Q_HEADS = 12
N_KV_HEADS_A = 4
N_MEM_HEADS = 4
MEM_LEN = 256
WINDOW = 128
GRID_W = 64
NA_ROWS = 8
NA_COLS = 16
D_FF = 2816
CONV_W = 3
EPS = 1e-6
NEG = -1e30
Q_WIDTH = N_Q_HEADS * HEAD_DIM
KV_WIDTH_A = N_KV_HEADS_A * HEAD_DIM
MEM_WIDTH = N_MEM_HEADS * HEAD_DIM
MIX_WIDTH = Q_WIDTH + MEM_WIDTH
LOG2E = 1.4426950408889634
Q_SCALE = HEAD_DIM ** -0.5 * LOG2E
REP_A = N_Q_HEADS // N_KV_HEADS_A

F32 = jnp.float32
BF16 = jnp.bfloat16

LANES = 128
VMEM_LIMIT_BYTES = 60 * 1024 * 1024
FF_CHUNK = 256
HALO = 16

_NT = (((1,), (1,)), ((), ()))


def _params(*sem):
    return pltpu.CompilerParams(dimension_semantics=sem, vmem_limit_bytes=VMEM_LIMIT_BYTES)


def _resident(shape, index_map):
    return pl.BlockSpec(shape, index_map, pipeline_mode=pl.Buffered(1))


def _rms(x, g):
    return x * lax.rsqrt(jnp.mean(x * x, axis=-1, keepdims=True) + EPS) * g


def _proj_kernel(x_ref, g_ref, w_ref, cs_ref, wt_ref, tok_ref, feat_ref):
    half = x_ref.shape[0] // 2
    for r in range(2):
        rows = slice(r * half, (r + 1) * half)
        h = _rms(x_ref[rows, :], g_ref[...]).astype(BF16)
        tok = jnp.dot(h, w_ref[...], preferred_element_type=F32) * cs_ref[...]
        tok_ref[rows, :] = tok.astype(tok_ref.dtype)
        feat_ref[0, :, rows] = lax.dot_general(wt_ref[...], h, _NT, preferred_element_type=F32).astype(feat_ref.dtype)


def _norm_project(x, g, w_tok, col_scale, w_feat_t, batch, tm):
    n, d = x.shape
    t = n // batch
    per_seq = t // tm
    wdt = w_tok.shape[1]
    wft = w_feat_t.shape[0]
    return pl.pallas_call(
        _proj_kernel,
        grid=(n // tm,),
        in_specs=[
            pl.BlockSpec((tm, d), lambda i: (i, 0)),
            _resident((1, d), lambda i: (0, 0)),
            _resident((d, wdt), lambda i: (0, 0)),
            _resident((1, wdt), lambda i: (0, 0)),
            _resident((wft, d), lambda i: (0, 0)),
        ],
        out_specs=[pl.BlockSpec((tm, wdt), lambda i: (i, 0)),
                   pl.BlockSpec((1, wft, tm), lambda i: (i // per_seq, 0, i % per_seq))],
        out_shape=[jax.ShapeDtypeStruct((n, wdt), BF16), jax.ShapeDtypeStruct((batch, wft, t), BF16)],
        compiler_params=_params("parallel"),
        name="norm_project",
    )(x, g.reshape(1, d), w_tok, col_scale.reshape(1, wdt), w_feat_t)


def _mem_proj_kernel(m_ref, g_ref, wv_ref, wkt_ref, *out_refs):
    x = m_ref[...]
    y = x * lax.rsqrt(jnp.mean(x * x, axis=-1, keepdims=True) + EPS)
    n_seq = m_ref.shape[0] // MEM_LEN
    for l in range(DEPTH):
        vm_ref, kmt_ref = out_refs[2 * l], out_refs[2 * l + 1]
        h = (y * g_ref[l:l + 1, :]).astype(BF16)
        vm_ref[...] = jnp.dot(h, wv_ref[l], preferred_element_type=F32).astype(vm_ref.dtype)
        for s in range(n_seq):
            kmt_ref[s] = lax.dot_general(wkt_ref[l], h[s * MEM_LEN:(s + 1) * MEM_LEN], _NT,
                                         preferred_element_type=F32).astype(kmt_ref.dtype)


def _memory_project(mem, g_mem, w_vm, w_kmt):
    b, _, d = mem.shape
    n_seq = next(c for c in (4, 2, 1) if b % c == 0)
    tm = n_seq * MEM_LEN
    outs = pl.pallas_call(
        _mem_proj_kernel,
        grid=(b // n_seq,),
        in_specs=[
            pl.BlockSpec((tm, d), lambda i: (i, 0)),
            _resident(g_mem.shape, lambda i: (0, 0)),
            _resident(w_vm.shape, lambda i: (0, 0, 0)),
            _resident(w_kmt.shape, lambda i: (0, 0, 0)),
        ],
        out_specs=[pl.BlockSpec((tm, MEM_WIDTH), lambda i: (i, 0)),
                   pl.BlockSpec((n_seq, MEM_WIDTH, MEM_LEN), lambda i: (i, 0, 0))] * DEPTH,
        out_shape=[jax.ShapeDtypeStruct((b * MEM_LEN, MEM_WIDTH), BF16),
                   jax.ShapeDtypeStruct((b, MEM_WIDTH, MEM_LEN), BF16)] * DEPTH,
        compiler_params=_params("parallel"),
        name="memory_project",
    )(mem.reshape(b * MEM_LEN, d), g_mem, w_vm, w_kmt)
    return [(outs[2 * l].reshape(b, MEM_LEN, MEM_WIDTH), outs[2 * l + 1]) for l in range(DEPTH)]


def _low_lanes(rows):
    return lax.broadcasted_iota(jnp.int32, (rows, LANES), 1) < HEAD_DIM


def _split_heads(q_tile, low):
    zero = jnp.zeros_like(q_tile)
    return jnp.where(low, q_tile, zero), jnp.where(low, zero, q_tile)


def _with_ones(v):
    return jnp.concatenate([v, jnp.ones_like(v)], axis=1)


def _merge_heads(oa, low, lo_rows, hi_rows, extra_den=None):
    pick = lambda a: jnp.where(low, a[lo_rows], a[hi_rows])
    den = pick(oa[:, LANES:])
    if extra_den is not None:
        den = den + pick(extra_den)
    return pick(oa[:, :LANES]) / den


def _run_pipelined(stages, ahead):
    pending = [stages[k][0]() for k in range(min(ahead, len(stages)))]
    for k, (_, finish) in enumerate(stages):
        if k + ahead < len(stages):
            pending.append(stages[k + ahead][0]())
        finish(pending.pop(0))


def _memory_stages(qm_ref, kmt_ref, vm_ref, o_ref, tq):
    low = _low_lanes(tq)
    stages = []
    for r in range(qm_ref.shape[1] // tq):
        rows = slice(r * tq, (r + 1) * tq)
        for p in range(MEM_WIDTH // LANES):
            cols = slice(p * LANES, (p + 1) * LANES)

            def scores(rows=rows, cols=cols):
                qs = jnp.concatenate(_split_heads(qm_ref[0, rows, cols], low), axis=0)
                return jnp.dot(qs, kmt_ref[0, cols, :], preferred_element_type=F32)

            def finish(s, rows=rows, cols=cols, p=p):
                pr = jnp.exp2(s - jnp.max(s, axis=-1, keepdims=True)).astype(BF16)
                oa = jnp.dot(pr, _with_ones(vm_ref[0, :, cols]), preferred_element_type=F32)
                o_ref[0, rows, Q_WIDTH + p * LANES:Q_WIDTH + (p + 1) * LANES] = (
                    _merge_heads(oa, low, slice(0, tq), slice(tq, 2 * tq)).astype(o_ref.dtype))

            stages.append((scores, finish))
    return stages


WINDOW_TILES_PER_STAGE = 1
WINDOW_LOOKAHEAD = 3


def _window_head(pair, row_block):
    return REP_A * (2 * pair + row_block % 2) + row_block // 2


def _window_head_order():
    return [_window_head(p, rb) for p in range(N_KV_HEADS_A // 2) for rb in range(2 * REP_A)]


def _window_bias_table():
    slopes = (2.0 ** (-8.0 * np.arange(1, N_Q_HEADS + 1, dtype=np.float32) / N_Q_HEADS)).astype(np.float32)
    blk = WINDOW
    qq = np.arange(blk)[:, None]
    kk = np.arange(3 * blk)[None, :]
    arel = np.abs(qq - kk + WINDOW)
    in_window = arel <= WINDOW
    in_seq = [kk >= blk, np.ones_like(kk, bool), kk < 2 * blk]
    out = np.empty((3, N_KV_HEADS_A // 2, 2 * REP_A, blk, 3 * blk), np.float32)
    for var in range(3):
        for p in range(N_KV_HEADS_A // 2):
            for rb in range(2 * REP_A):
                alibi = -slopes[_window_head(p, rb)] * arel.astype(np.float32) * np.float32(LOG2E)
                out[var, p, rb] = np.where(in_window & in_seq[var], alibi, np.float32(NEG))
    return out.reshape(3 * (N_KV_HEADS_A // 2), 2 * REP_A * blk, 3 * blk)


def _window_kernel(sink_ref, q_ref, ktp_ref, ktc_ref, ktn_ref, vp_ref, vc_ref, vn_ref, qm_ref, kmt_ref, vm_ref,
                   bias_ref, o_ref, ktbuf, vbuf, *, tq, n_blk):
    i = pl.program_id(1)
    blk = WINDOW
    n_pair = N_KV_HEADS_A // 2
    ktbuf[:, 0:blk] = ktp_ref[0]
    ktbuf[:, blk:blk + tq] = ktc_ref[0]
    ktbuf[:, blk + tq:] = ktn_ref[0]
    vbuf[0:blk] = vp_ref[0]
    vbuf[blk:blk + tq] = vc_ref[0]
    vbuf[blk + tq:] = vn_ref[0]

    low = _low_lanes(blk)
    sinks = [jnp.concatenate([jnp.full((blk, LANES), sink_ref[_window_head(p, rb)], F32)
                              for rb in range(2 * REP_A)], axis=0) for p in range(n_pair)]
    nt = WINDOW_TILES_PER_STAGE

    def scores(j, p, t0):
        g_blk = i * (tq // blk) + j
        variant = jnp.where(g_blk == 0, 0, jnp.where(g_blk == n_blk - 1, 2, 1))
        tiles = []
        for t in range(t0, t0 + nt):
            c0 = (REP_A * p + t) * LANES
            tiles.extend(_split_heads(q_ref[0, j * blk:(j + 1) * blk, c0:c0 + LANES], low))
        qs = jnp.concatenate(tiles, axis=0)
        sc = jnp.dot(qs, ktbuf[p * LANES:(p + 1) * LANES, j * blk:(j + 3) * blk], preferred_element_type=F32)
        return sc + bias_ref[variant * n_pair + p, 2 * t0 * blk:2 * (t0 + nt) * blk, :]

    def finish(j, p, t0, sc):
        sink = sinks[p][2 * t0 * blk:2 * (t0 + nt) * blk]
        m = jnp.maximum(jnp.max(sc, axis=-1, keepdims=True), sink)
        pr = jnp.concatenate([jnp.exp2(sc[:, c * LANES:(c + 1) * LANES] - m) for c in range(3 * blk // LANES)],
                             axis=1).astype(BF16)
        oa = jnp.dot(pr, _with_ones(vbuf[j * blk:(j + 3) * blk, p * LANES:(p + 1) * LANES]),
                     preferred_element_type=F32)
        sink_mass = jnp.exp2(sink - m)
        for k in range(nt):
            c0 = (REP_A * p + t0 + k) * LANES
            o_ref[0, j * blk:(j + 1) * blk, c0:c0 + LANES] = _merge_heads(
                oa, low, slice(2 * k * blk, (2 * k + 1) * blk), slice((2 * k + 1) * blk, (2 * k + 2) * blk),
                extra_den=sink_mass).astype(o_ref.dtype)

    stages = [(functools.partial(scores, j, p, t0), functools.partial(finish, j, p, t0))
              for j in range(tq // blk) for p in range(n_pair) for t0 in range(0, REP_A, nt)]
    _run_pipelined(stages + _memory_stages(qm_ref, kmt_ref, vm_ref, o_ref, 2 * blk), WINDOW_LOOKAHEAD)


def _window_attention(tok, kt, kmt, vm, sink, bias, tq):
    b, t, _ = tok.shape
    blk = WINDOW
    n_blk = t // blk
    per = tq // blk
    assert n_blk >= 2 and t % tq == 0
    vcol = Q_WIDTH // KV_WIDTH_A
    prev_i = lambda i: jnp.maximum(i * per - 1, 0)
    next_i = lambda i: jnp.minimum((i + 1) * per, n_blk - 1)
    kernel = functools.partial(_window_kernel, tq=tq, n_blk=n_blk)
    return pl.pallas_call(
        kernel,
        grid=(b, t // tq),
        in_specs=[
            pl.BlockSpec(memory_space=pltpu.SMEM),
            pl.BlockSpec((1, tq, Q_WIDTH), lambda bi, i: (bi, i, 0)),
            pl.BlockSpec((1, KV_WIDTH_A, blk), lambda bi, i: (bi, 0, prev_i(i))),
            pl.BlockSpec((1, KV_WIDTH_A, tq), lambda bi, i: (bi, 0, i)),
            pl.BlockSpec((1, KV_WIDTH_A, blk), lambda bi, i: (bi, 0, next_i(i))),
            pl.BlockSpec((1, blk, KV_WIDTH_A), lambda bi, i: (bi, prev_i(i), vcol)),
            pl.BlockSpec((1, tq, KV_WIDTH_A), lambda bi, i: (bi, i, vcol)),
            pl.BlockSpec((1, blk, KV_WIDTH_A), lambda bi, i: (bi, next_i(i), vcol)),
            pl.BlockSpec((1, tq, MEM_WIDTH), lambda bi, i: (bi, i, vcol + 1)),
            pl.BlockSpec((1, MEM_WIDTH, MEM_LEN), lambda bi, i: (bi, 0, 0)),
            pl.BlockSpec((1, MEM_LEN, MEM_WIDTH), lambda bi, i: (bi, 0, 0)),
            _resident(bias.shape, lambda bi, i: (0, 0, 0)),
        ],
        out_specs=pl.BlockSpec((1, tq, MIX_WIDTH), lambda bi, i: (bi, i, 0)),
        out_shape=jax.ShapeDtypeStruct((b, t, MIX_WIDTH), BF16),
        scratch_shapes=[pltpu.VMEM((KV_WIDTH_A, tq + 2 * blk), BF16),
                        pltpu.VMEM((tq + 2 * blk, KV_WIDTH_A), BF16)],
        compiler_params=_params("parallel", "parallel"),
        name="window_attention",
    )(sink, tok, kt, kt, kt, tok, tok, tok, tok, kmt, vm, bias)


NB_ROWS_PER_TILE = 4
NB_TQ = NB_ROWS_PER_TILE * GRID_W
NB_TILES_PER_STEP = 4
NB_LOOKAHEAD = 2


def _nbr_bias_table(rpb):
    h, n_dr, n_dc = rpb.shape
    w = GRID_W
    lo = (w - 1) - (NA_COLS - 1)
    e = jnp.pad(rpb.astype(F32), ((0, 0), (0, 0), (lo, 2 * w - lo - n_dc)))
    toe = jnp.tile(e, (1, 1, w))[..., :w * (2 * w - 1)].reshape(h, n_dr, w, 2 * w - 1)
    blocks = toe[..., w - 1:]
    zero = jnp.zeros((h, w, w), F32)
    rows = []
    for a in range(NB_ROWS_PER_TILE):
        strip = []
        for b in range(3 * NB_ROWS_PER_TILE):
            dr = (b - NB_ROWS_PER_TILE) - a + (NA_ROWS - 1)
            strip.append(blocks[:, dr] if 0 <= dr < n_dr else zero)
        rows.append(jnp.concatenate(strip, axis=-1))
    return jnp.concatenate(rows, axis=1)


def _nbr_mask_table(n_tiles, n_rows):
    qi = np.arange(NB_TQ)[:, None]
    ki = np.arange(3 * NB_TQ)[None, :]
    c0 = np.clip(qi % GRID_W - NA_COLS // 2, 0, GRID_W - NA_COLS)
    k_col = ki % GRID_W
    col_ok = (k_col >= c0) & (k_col < c0 + NA_COLS)
    out = []
    for i in (0, min(1, n_tiles - 1), n_tiles - 1):
        q_row = i * NB_ROWS_PER_TILE + qi // GRID_W
        k_row = (i - 1) * NB_ROWS_PER_TILE + ki // GRID_W
        r0 = np.clip(q_row - NA_ROWS // 2, 0, n_rows - NA_ROWS)
        row_ok = (k_row >= r0) & (k_row < r0 + NA_ROWS)
        out.append(np.where(row_ok & col_ok, np.float32(0.0), np.float32(NEG)))
    return np.stack(out).astype(np.float32)


_NBR_CANONICAL_TILES = 3


def _nbr_logit_table(rpb):
    mask = _nbr_mask_table(_NBR_CANONICAL_TILES, _NBR_CANONICAL_TILES * NB_ROWS_PER_TILE)
    table = (_nbr_bias_table(rpb)[None] + jnp.asarray(mask)[:, None]) * LOG2E
    return table.reshape(3 * N_Q_HEADS, NB_TQ, 3 * NB_TQ)


def _nbr_kernel(q_ref, ktp_ref, ktc_ref, ktn_ref, vp_ref, vc_ref, vn_ref, qm_ref, kmt_ref, vm_ref, table_ref,
                o_ref, *, n_tiles):
    i = pl.program_id(1)
    tq = NB_TQ
    per = NB_TILES_PER_STEP
    low = _low_lanes(tq)
    kt_tiles = ([lambda cols: ktp_ref[0, cols, :]]
                + [(lambda cols, s=s: ktc_ref[0, cols, s * tq:(s + 1) * tq]) for s in range(per)]
                + [lambda cols: ktn_ref[0, cols, :]])
    v_tiles = ([lambda cols: vp_ref[0, :, cols]]
               + [(lambda cols, s=s: vc_ref[0, s * tq:(s + 1) * tq, cols]) for s in range(per)]
               + [lambda cols: vn_ref[0, :, cols]])
    def scores(s, p, hi):
        tile = i * per + s
        variant = jnp.where(tile == 0, 0, jnp.where(tile == n_tiles - 1, 2, 1))
        cols = slice(p * LANES, (p + 1) * LANES)
        qs = _split_heads(q_ref[0, s * tq:(s + 1) * tq, cols], low)[hi]
        kt = jnp.concatenate([f(cols) for f in kt_tiles[s:s + 3]], axis=1)
        sc = jnp.dot(qs, kt, preferred_element_type=F32)
        return sc + table_ref[variant * N_Q_HEADS + 2 * p + hi]

    pending = {}

    def finish(s, p, hi, sc):
        cols = slice(p * LANES, (p + 1) * LANES)
        pr = jnp.exp2(sc - jnp.max(sc, axis=-1, keepdims=True)).astype(BF16)
        v = jnp.concatenate([f(cols) for f in v_tiles[s:s + 3]], axis=0)
        oa = jnp.dot(pr, _with_ones(v), preferred_element_type=F32)
        if not hi:
            pending[(s, p)] = oa
            return
        both = jnp.concatenate([pending.pop((s, p)), oa], axis=0)
        o_ref[0, s * tq:(s + 1) * tq, cols] = (
            _merge_heads(both, low, slice(0, tq), slice(tq, 2 * tq)).astype(o_ref.dtype))

    stages = [(functools.partial(scores, s, p, hi), functools.partial(finish, s, p, hi))
              for s in range(per) for p in range(Q_WIDTH // LANES) for hi in range(2)]
    _run_pipelined(stages + _memory_stages(qm_ref, kmt_ref, vm_ref, o_ref, tq), NB_LOOKAHEAD)


def _nbr_attention(tok, kt, kmt, vm, table):
    b, t, _ = tok.shape
    tq = NB_TQ
    n = t // tq
    n_rows = t // GRID_W
    per = NB_TILES_PER_STEP
    ts = per * tq
    assert n_rows >= NA_ROWS and t % ts == 0 and n >= 2
    canonical = _nbr_mask_table(_NBR_CANONICAL_TILES, _NBR_CANONICAL_TILES * NB_ROWS_PER_TILE)
    assert n == 2 or np.array_equal(_nbr_mask_table(n, n_rows), canonical)
    assert n > 2 or np.array_equal(_nbr_mask_table(n, n_rows)[[0, 2]], canonical[[0, 2]])
    prev_i = lambda i: jnp.maximum(i * per - 1, 0)
    next_i = lambda i: jnp.minimum((i + 1) * per, n - 1)
    return pl.pallas_call(
        functools.partial(_nbr_kernel, n_tiles=n),
        grid=(b, t // ts),
        in_specs=[
            pl.BlockSpec((1, ts, Q_WIDTH), lambda bi, i: (bi, i, 0)),
            pl.BlockSpec((1, Q_WIDTH, tq), lambda bi, i: (bi, 0, prev_i(i))),
            pl.BlockSpec((1, Q_WIDTH, ts), lambda bi, i: (bi, 0, i)),
            pl.BlockSpec((1, Q_WIDTH, tq), lambda bi, i: (bi, 0, next_i(i))),
            pl.BlockSpec((1, tq, Q_WIDTH), lambda bi, i: (bi, prev_i(i), 1)),
            pl.BlockSpec((1, ts, Q_WIDTH), lambda bi, i: (bi, i, 1)),
            pl.BlockSpec((1, tq, Q_WIDTH), lambda bi, i: (bi, next_i(i), 1)),
            pl.BlockSpec((1, ts, MEM_WIDTH), lambda bi, i: (bi, i, 2 * Q_WIDTH // MEM_WIDTH)),
            pl.BlockSpec((1, MEM_WIDTH, MEM_LEN), lambda bi, i: (bi, 0, 0)),
            pl.BlockSpec((1, MEM_LEN, MEM_WIDTH), lambda bi, i: (bi, 0, 0)),
            _resident(table.shape, lambda bi, i: (0, 0, 0)),
        ],
        out_specs=pl.BlockSpec((1, ts, MIX_WIDTH), lambda bi, i: (bi, i, 0)),
        out_shape=jax.ShapeDtypeStruct((b, t, MIX_WIDTH), BF16),
        compiler_params=_params("parallel", "parallel"),
        name="nbr_attention",
    )(tok, kt, kt, kt, tok, tok, tok, tok, kmt, vm, table)


def _post_kernel(xp_ref, xc_ref, xn_ref, op_ref, oc_ref, on_ref, wo_ref, gn_ref, wg_ref, wu_ref, cw_ref, cb_ref,
                 wd_ref, gf_ref, y_ref, oext_ref, hext_ref, act_ref, *, final_norm):
    i = pl.program_id(1)
    n = pl.num_programs(1)
    tm = xc_ref.shape[1]
    ext = tm + HALO
    halo_row = lax.broadcasted_iota(jnp.int32, (HALO, 1), 0)
    take_next = halo_row < HALO // 2
    oext_ref[0:tm] = oc_ref[0]
    oext_ref[tm:] = jnp.where(take_next, on_ref[0], op_ref[0])
    mix = jnp.dot(oext_ref[...], wo_ref[...], preferred_element_type=F32)
    gn = gn_ref[...]
    x1 = xc_ref[0] + mix[0:tm]
    y_ref[0] = x1
    hext_ref[0:tm] = _rms(x1, gn).astype(BF16)
    h_halo = _rms(jnp.where(take_next, xn_ref[0], xp_ref[0]) + mix[tm:], gn).astype(BF16)
    halo_ok = ((halo_row == 0) & (i < n - 1)) | ((halo_row == HALO - 1) & (i > 0))
    hext_ref[tm:] = jnp.where(halo_ok, h_halo, jnp.zeros_like(h_halo))

    for c in range(D_FF // FF_CHUNK):
        cols = slice(c * FF_CHUNK, (c + 1) * FF_CHUNK)
        g_ext = jnp.dot(hext_ref[...], wg_ref[:, cols], preferred_element_type=F32)
        g = g_ext[0:tm]
        g_prev = pltpu.roll(g_ext, 1, 0)[0:tm]
        g_next = pltpu.roll(g_ext, ext - 1, 0)[0:tm]
        gc = g_prev * cw_ref[0:1, cols] + g * cw_ref[1:2, cols] + g_next * cw_ref[2:3, cols] + cb_ref[:, cols]
        u = jnp.dot(hext_ref[0:tm], wu_ref[:, cols], preferred_element_type=F32)
        act_ref[:, cols] = (gc * jax.nn.sigmoid(gc) * u).astype(BF16)

    y = y_ref[0] + jnp.dot(act_ref[...], wd_ref[...], preferred_element_type=F32)
    if final_norm:
        y = _rms(y, gf_ref[...])
    y_ref[0] = y


def _post_attention(x, o, w_o, g_ffn, w_gate, w_up, conv_w, conv_b, w_down, g_final, tm, final_norm):
    b, t, d = x.shape
    per = tm // HALO
    n_halo = t // HALO
    kernel = functools.partial(_post_kernel, final_norm=final_norm)
    const2 = lambda bi, i: (0, 0)
    prev_map = lambda bi, i: (bi, jnp.maximum(i * per - 1, 0), 0)
    cur_map = lambda bi, i: (bi, i, 0)
    next_map = lambda bi, i: (bi, jnp.minimum((i + 1) * per, n_halo - 1), 0)
    return pl.pallas_call(
        kernel,
        grid=(b, t // tm),
        in_specs=[
            pl.BlockSpec((1, HALO, d), prev_map),
            pl.BlockSpec((1, tm, d), cur_map),
            pl.BlockSpec((1, HALO, d), next_map),
            pl.BlockSpec((1, HALO, MIX_WIDTH), prev_map),
            pl.BlockSpec((1, tm, MIX_WIDTH), cur_map),
            pl.BlockSpec((1, HALO, MIX_WIDTH), next_map),
            _resident(w_o.shape, const2),
            _resident((1, d), const2),
            _resident(w_gate.shape, const2),
            _resident(w_up.shape, const2),
            _resident(conv_w.shape, const2),
            _resident(conv_b.shape, const2),
            _resident(w_down.shape, const2),
            _resident((1, d), const2),
        ],
        out_specs=pl.BlockSpec((1, tm, d), cur_map),
        out_shape=jax.ShapeDtypeStruct((b, t, d), F32),
        scratch_shapes=[pltpu.VMEM((tm + HALO, MIX_WIDTH), BF16), pltpu.VMEM((tm + HALO, d), BF16),
                        pltpu.VMEM((tm, D_FF), BF16)],
        compiler_params=_params("parallel", "parallel"),
        name="post_attention",
    )(x, x, x, o, o, o, w_o, g_ffn.reshape(1, d), w_gate, w_up, conv_w, conv_b, w_down, g_final.reshape(1, d))


def _prepare_weights(g_mix, g_mem, w_in_a, sink_a, w_in_b, rpb_b, w_mem_kv, w_o, g_ffn, w_gate, w_up, conv_w,
                     conv_b, w_down, g_final):
    head_cols = np.arange(Q_WIDTH).reshape(N_Q_HEADS, HEAD_DIM)
    perm_a = head_cols[_window_head_order()].reshape(-1)
    layers = []
    for i in range(DEPTH):
        j = i // 2
        wo = w_o[i]
        if i % 2 == 0:
            w = w_in_a[j]
            wq, wk, wv, wqm = (w[:, :Q_WIDTH], w[:, Q_WIDTH:Q_WIDTH + KV_WIDTH_A],
                               w[:, Q_WIDTH + KV_WIDTH_A:Q_WIDTH + 2 * KV_WIDTH_A], w[:, Q_WIDTH + 2 * KV_WIDTH_A:])
            wq = wq[:, perm_a]
            wo = jnp.concatenate([wo[:Q_WIDTH][perm_a], wo[Q_WIDTH:]], axis=0)
            extra = dict(sink=sink_a[j].astype(F32) * LOG2E, bias=jnp.asarray(_window_bias_table()))
        else:
            w = w_in_b[j]
            wq, wk, wv, wqm = (w[:, :Q_WIDTH], w[:, Q_WIDTH:2 * Q_WIDTH], w[:, 2 * Q_WIDTH:3 * Q_WIDTH],
                               w[:, 3 * Q_WIDTH:])
            extra = dict(bias=_nbr_logit_table(rpb_b[j]))
        tok_scale = np.concatenate([np.full(wq.shape[1], Q_SCALE), np.ones(wv.shape[1]),
                                    np.full(wqm.shape[1], Q_SCALE)]).astype(np.float32)
        layers.append(dict(
            g_mix=g_mix[i], g_ffn=g_ffn[i],
            w_tok=jnp.concatenate([wq, wv, wqm], axis=1).astype(BF16), tok_scale=jnp.asarray(tok_scale),
            w_kt=wk.T.astype(BF16),
            w_o=wo.astype(BF16),
            w_gate=w_gate[i].astype(BF16), w_up=w_up[i].astype(BF16),
            conv_w=conv_w[i], conv_b=conv_b[i].reshape(1, D_FF),
            w_down=w_down[i].astype(BF16), **extra))
    mem_weights = dict(g_mem=g_mem, w_vm=w_mem_kv[:, :, MEM_WIDTH:].astype(BF16),
                       w_kmt=jnp.swapaxes(w_mem_kv[:, :, :MEM_WIDTH], 1, 2).astype(BF16))
    return layers, mem_weights


def _trunk(x, mem, layers, mem_weights, g_final):
    b, t, d = x.shape
    n = b * t
    tm = 512
    mem_kv = _memory_project(mem, mem_weights["g_mem"], mem_weights["w_vm"], mem_weights["w_kmt"])
    for i, lw in enumerate(layers):
        tok, kt = _norm_project(x.reshape(n, d), lw["g_mix"], lw["w_tok"], lw["tok_scale"], lw["w_kt"], b, 2 * tm)
        tok = tok.reshape(b, t, -1)
        vm, kmt = mem_kv[i]
        if i % 2 == 0:
            o = _window_attention(tok, kt, kmt, vm, lw["sink"], lw["bias"], tq=1024)
        else:
            o = _nbr_attention(tok, kt, kmt, vm, lw["bias"])
        x = _post_attention(x, o, lw["w_o"], lw["g_ffn"], lw["w_gate"], lw["w_up"], lw["conv_w"], lw["conv_b"],
                            lw["w_down"], g_final, 2 * tm, final_norm=(i == DEPTH - 1))
    return x


def kernel(x_prompt, x_sample, mem_prompt, mem_sample, g_mix, g_mem, w_in_a, sink_a, w_in_b, rpb_b, w_mem_kv, w_o,
           g_ffn, w_gate, w_up, conv_w, conv_b, w_down, g_final):
    layers, mem_weights = _prepare_weights(g_mix, g_mem, w_in_a, sink_a, w_in_b, rpb_b, w_mem_kv, w_o, g_ffn, w_gate,
                                           w_up, conv_w, conv_b, w_down, g_final)
    y_prompt = _trunk(x_prompt, mem_prompt, layers, mem_weights, g_final)
    y_sample = _trunk(x_sample, mem_sample, layers, mem_weights, g_final)
    return (y_prompt, y_sample)
```

```python
import functools

import numpy as np
import jax
import jax.numpy as jnp
from jax import lax
from jax.experimental import pallas as pl
from jax.experimental.pallas import tpu as pltpu

D_MODEL = 1024
DEPTH = 2
HEAD_DIM = 64
N_Q_HEADS = 12
N_KV_HEADS_A = 4
N_MEM_HEADS = 4
MEM_LEN = 256
WINDOW = 128
GRID_W = 64
NA_ROWS = 8
NA_COLS = 16
D_FF = 2816
CONV_W = 3
EPS = 1e-6
NEG = -1e30
Q_WIDTH = N_Q_HEADS * HEAD_DIM
KV_WIDTH_A = N_KV_HEADS_A * HEAD_DIM
MEM_WIDTH = N_MEM_HEADS * HEAD_DIM
MIX_WIDTH = Q_WIDTH + MEM_WIDTH
LOG2E = 1.4426950408889634
Q_SCALE = HEAD_DIM ** -0.5 * LOG2E
REP_A = N_Q_HEADS // N_KV_HEADS_A

F32 = jnp.float32
BF16 = jnp.bfloat16

LANES = 128
VMEM_LIMIT_BYTES = 60 * 1024 * 1024
MXU_WIDTH = 256
FF_CHUNK_BOUNDS = (0, 6 * MXU_WIDTH, D_FF)
HALO = 16

_NT = (((1,), (1,)), ((), ()))


def _params(*sem):
    return pltpu.CompilerParams(dimension_semantics=sem, vmem_limit_bytes=VMEM_LIMIT_BYTES)


def _resident(shape, index_map):
    return pl.BlockSpec(shape, index_map, pipeline_mode=pl.Buffered(1))


def _rms(x, g):
    return x * lax.rsqrt(jnp.mean(x * x, axis=-1, keepdims=True) + EPS) * g


def _proj_kernel(x_ref, g_ref, w_ref, cs_ref, wt_ref, tok_ref, feat_ref):
    half = x_ref.shape[0] // 2
    for r in range(2):
        rows = slice(r * half, (r + 1) * half)
        h = _rms(x_ref[rows, :], g_ref[...]).astype(BF16)
        tok = jnp.dot(h, w_ref[...], preferred_element_type=F32) * cs_ref[...]
        tok_ref[rows, :] = tok.astype(tok_ref.dtype)
        feat_ref[0, :, rows] = lax.dot_general(wt_ref[...], h, _NT, preferred_element_type=F32).astype(feat_ref.dtype)


def _norm_project(x, g, w_tok, col_scale, w_feat_t, batch, tm):
    n, d = x.shape
    t = n // batch
    per_seq = t // tm
    wdt = w_tok.shape[1]
    wft = w_feat_t.shape[0]
    return pl.pallas_call(
        _proj_kernel,
        grid=(n // tm,),
        in_specs=[
            pl.BlockSpec((tm, d), lambda i: (i, 0)),
            _resident((1, d), lambda i: (0, 0)),
            _resident((d, wdt), lambda i: (0, 0)),
            _resident((1, wdt), lambda i: (0, 0)),
            _resident((wft, d), lambda i: (0, 0)),
        ],
        out_specs=[pl.BlockSpec((tm, wdt), lambda i: (i, 0)),
                   pl.BlockSpec((1, wft, tm), lambda i: (i // per_seq, 0, i % per_seq))],
        out_shape=[jax.ShapeDtypeStruct((n, wdt), BF16), jax.ShapeDtypeStruct((batch, wft, t), BF16)],
        compiler_params=_params("parallel"),
        name="norm_project",
    )(x, g.reshape(1, d), w_tok, col_scale.reshape(1, wdt), w_feat_t)


def _mem_proj_kernel(m_ref, g_ref, wv_ref, wkt_ref, *out_refs):
    x = m_ref[...]
    y = x * lax.rsqrt(jnp.mean(x * x, axis=-1, keepdims=True) + EPS)
    n_seq = m_ref.shape[0] // MEM_LEN
    for l in range(DEPTH):
        vm_ref, kmt_ref = out_refs[2 * l], out_refs[2 * l + 1]
        h = (y * g_ref[l:l + 1, :]).astype(BF16)
        vm_ref[...] = jnp.dot(h, wv_ref[l], preferred_element_type=F32).astype(vm_ref.dtype)
        for s in range(n_seq):
            kmt_ref[s] = lax.dot_general(wkt_ref[l], h[s * MEM_LEN:(s + 1) * MEM_LEN], _NT,
                                         preferred_element_type=F32).astype(kmt_ref.dtype)


def _memory_project(mem, g_mem, w_vm, w_kmt):
    b, _, d = mem.shape
    n_seq = next(c for c in (4, 2, 1) if b % c == 0)
    tm = n_seq * MEM_LEN
    outs = pl.pallas_call(
        _mem_proj_kernel,
        grid=(b // n_seq,),
        in_specs=[
            pl.BlockSpec((tm, d), lambda i: (i, 0)),
            _resident(g_mem.shape, lambda i: (0, 0)),
            _resident(w_vm.shape, lambda i: (0, 0, 0)),
            _resident(w_kmt.shape, lambda i: (0, 0, 0)),
        ],
        out_specs=[pl.BlockSpec((tm, MEM_WIDTH), lambda i: (i, 0)),
                   pl.BlockSpec((n_seq, MEM_WIDTH, MEM_LEN), lambda i: (i, 0, 0))] * DEPTH,
        out_shape=[jax.ShapeDtypeStruct((b * MEM_LEN, MEM_WIDTH), BF16),
                   jax.ShapeDtypeStruct((b, MEM_WIDTH, MEM_LEN), BF16)] * DEPTH,
        compiler_params=_params("parallel"),
        name="memory_project",
    )(mem.reshape(b * MEM_LEN, d), g_mem, w_vm, w_kmt)
    return [(outs[2 * l].reshape(b, MEM_LEN, MEM_WIDTH), outs[2 * l + 1]) for l in range(DEPTH)]


def _low_lanes(rows):
    return lax.broadcasted_iota(jnp.int32, (rows, LANES), 1) < HEAD_DIM


def _split_heads(q_tile, low):
    zero = jnp.zeros_like(q_tile)
    return jnp.where(low, q_tile, zero), jnp.where(low, zero, q_tile)


def _with_ones(v):
    return jnp.concatenate([v, jnp.ones_like(v)], axis=1)


def _merge_heads(oa, low, lo_rows, hi_rows, extra_den=None):
    pick = lambda a: jnp.where(low, a[lo_rows], a[hi_rows])
    den = pick(oa[:, LANES:])
    if extra_den is not None:
        den = den + pick(extra_den)
    return pick(oa[:, :LANES]) / den


def _run_pipelined(stages, ahead):
    pending = [stages[k][0]() for k in range(min(ahead, len(stages)))]
    for k, (_, finish) in enumerate(stages):
        if k + ahead < len(stages):
            pending.append(stages[k + ahead][0]())
        finish(pending.pop(0))


def _memory_stages(qm_ref, kmt_ref, vm_ref, o_ref, tq):
    low = _low_lanes(tq)
    stages = []
    for r in range(qm_ref.shape[1] // tq):
        rows = slice(r * tq, (r + 1) * tq)
        for p in range(MEM_WIDTH // LANES):
            cols = slice(p * LANES, (p + 1) * LANES)

            def scores(rows=rows, cols=cols):
                qs = jnp.concatenate(_split_heads(qm_ref[0, rows, cols], low), axis=0)
                return jnp.dot(qs, kmt_ref[0, cols, :], preferred_element_type=F32)

            def finish(s, rows=rows, cols=cols, p=p):
                pr = jnp.exp2(s - jnp.max(s, axis=-1, keepdims=True)).astype(BF16)
                oa = jnp.dot(pr, _with_ones(vm_ref[0, :, cols]), preferred_element_type=F32)
                o_ref[0, rows, Q_WIDTH + p * LANES:Q_WIDTH + (p + 1) * LANES] = (
                    _merge_heads(oa, low, slice(0, tq), slice(tq, 2 * tq)).astype(o_ref.dtype))

            stages.append((scores, finish))
    return stages


WINDOW_TILES_PER_STAGE = 1
WINDOW_LOOKAHEAD = 3


def _window_head(pair, row_block):
    return REP_A * (2 * pair + row_block % 2) + row_block // 2


def _window_head_order():
    return [_window_head(p, rb) for p in range(N_KV_HEADS_A // 2) for rb in range(2 * REP_A)]


def _window_bias_table():
    slopes = (2.0 ** (-8.0 * np.arange(1, N_Q_HEADS + 1, dtype=np.float32) / N_Q_HEADS)).astype(np.float32)
    blk = WINDOW
    qq = np.arange(blk)[:, None]
    kk = np.arange(3 * blk)[None, :]
    arel = np.abs(qq - kk + WINDOW)
    in_window = arel <= WINDOW
    in_seq = [kk >= blk, np.ones_like(kk, bool), kk < 2 * blk]
    out = np.empty((3, N_KV_HEADS_A // 2, 2 * REP_A, blk, 3 * blk), np.float32)
    for var in range(3):
        for p in range(N_KV_HEADS_A // 2):
            for rb in range(2 * REP_A):
                alibi = -slopes[_window_head(p, rb)] * arel.astype(np.float32) * np.float32(LOG2E)
                out[var, p, rb] = np.where(in_window & in_seq[var], alibi, np.float32(NEG))
    return out.reshape(3 * (N_KV_HEADS_A // 2), 2 * REP_A * blk, 3 * blk)


def _window_kernel(sink_ref, q_ref, ktp_ref, ktc_ref, ktn_ref, vp_ref, vc_ref, vn_ref, qm_ref, kmt_ref, vm_ref,
                   bias_ref, o_ref, ktbuf, vbuf, *, tq, n_blk):
    i = pl.program_id(1)
    blk = WINDOW
    n_pair = N_KV_HEADS_A // 2
    ktbuf[:, 0:blk] = ktp_ref[0]
    ktbuf[:, blk:blk + tq] = ktc_ref[0]
    ktbuf[:, blk + tq:] = ktn_ref[0]
    vbuf[0:blk] = vp_ref[0]
    vbuf[blk:blk + tq] = vc_ref[0]
    vbuf[blk + tq:] = vn_ref[0]

    low = _low_lanes(blk)
    sinks = [jnp.concatenate([jnp.full((blk, LANES), sink_ref[_window_head(p, rb)], F32)
                              for rb in range(2 * REP_A)], axis=0) for p in range(n_pair)]
    nt = WINDOW_TILES_PER_STAGE

    def scores(j, p, t0):
        g_blk = i * (tq // blk) + j
        variant = jnp.where(g_blk == 0, 0, jnp.where(g_blk == n_blk - 1, 2, 1))
        tiles = []
        for t in range(t0, t0 + nt):
            c0 = (REP_A * p + t) * LANES
            tiles.extend(_split_heads(q_ref[0, j * blk:(j + 1) * blk, c0:c0 + LANES], low))
        qs = jnp.concatenate(tiles, axis=0)
        sc = jnp.dot(qs, ktbuf[p * LANES:(p + 1) * LANES, j * blk:(j + 3) * blk], preferred_element_type=F32)
        return sc + bias_ref[variant * n_pair + p, 2 * t0 * blk:2 * (t0 + nt) * blk, :]

    def finish(j, p, t0, sc):
        sink = sinks[p][2 * t0 * blk:2 * (t0 + nt) * blk]
        m = jnp.maximum(jnp.max(sc, axis=-1, keepdims=True), sink)
        pr = jnp.concatenate([jnp.exp2(sc[:, c * LANES:(c + 1) * LANES] - m) for c in range(3 * blk // LANES)],
                             axis=1).astype(BF16)
        oa = jnp.dot(pr, _with_ones(vbuf[j * blk:(j + 3) * blk, p * LANES:(p + 1) * LANES]),
                     preferred_element_type=F32)
        sink_mass = jnp.exp2(sink - m)
        for k in range(nt):
            c0 = (REP_A * p + t0 + k) * LANES
            o_ref[0, j * blk:(j + 1) * blk, c0:c0 + LANES] = _merge_heads(
                oa, low, slice(2 * k * blk, (2 * k + 1) * blk), slice((2 * k + 1) * blk, (2 * k + 2) * blk),
                extra_den=sink_mass).astype(o_ref.dtype)

    stages = [(functools.partial(scores, j, p, t0), functools.partial(finish, j, p, t0))
              for j in range(tq // blk) for p in range(n_pair) for t0 in range(0, REP_A, nt)]
    _run_pipelined(stages + _memory_stages(qm_ref, kmt_ref, vm_ref, o_ref, 2 * blk), WINDOW_LOOKAHEAD)


def _window_attention(tok, kt, kmt, vm, sink, bias, tq):
    b, t, _ = tok.shape
    blk = WINDOW
    n_blk = t // blk
    per = tq // blk
    assert n_blk >= 2 and t % tq == 0
    vcol = Q_WIDTH // KV_WIDTH_A
    prev_i = lambda i: jnp.maximum(i * per - 1, 0)
    next_i = lambda i: jnp.minimum((i + 1) * per, n_blk - 1)
    kernel = functools.partial(_window_kernel, tq=tq, n_blk=n_blk)
    return pl.pallas_call(
        kernel,
        grid=(b, t // tq),
        in_specs=[
            pl.BlockSpec(memory_space=pltpu.SMEM),
            pl.BlockSpec((1, tq, Q_WIDTH), lambda bi, i: (bi, i, 0)),
            pl.BlockSpec((1, KV_WIDTH_A, blk), lambda bi, i: (bi, 0, prev_i(i))),
            pl.BlockSpec((1, KV_WIDTH_A, tq), lambda bi, i: (bi, 0, i)),
            pl.BlockSpec((1, KV_WIDTH_A, blk), lambda bi, i: (bi, 0, next_i(i))),
            pl.BlockSpec((1, blk, KV_WIDTH_A), lambda bi, i: (bi, prev_i(i), vcol)),
            pl.BlockSpec((1, tq, KV_WIDTH_A), lambda bi, i: (bi, i, vcol)),
            pl.BlockSpec((1, blk, KV_WIDTH_A), lambda bi, i: (bi, next_i(i), vcol)),
            pl.BlockSpec((1, tq, MEM_WIDTH), lambda bi, i: (bi, i, vcol + 1)),
            pl.BlockSpec((1, MEM_WIDTH, MEM_LEN), lambda bi, i: (bi, 0, 0)),
            pl.BlockSpec((1, MEM_LEN, MEM_WIDTH), lambda bi, i: (bi, 0, 0)),
            _resident(bias.shape, lambda bi, i: (0, 0, 0)),
        ],
        out_specs=pl.BlockSpec((1, tq, MIX_WIDTH), lambda bi, i: (bi, i, 0)),
        out_shape=jax.ShapeDtypeStruct((b, t, MIX_WIDTH), BF16),
        scratch_shapes=[pltpu.VMEM((KV_WIDTH_A, tq + 2 * blk), BF16),
                        pltpu.VMEM((tq + 2 * blk, KV_WIDTH_A), BF16)],
        compiler_params=_params("parallel", "parallel"),
        name="window_attention",
    )(sink, tok, kt, kt, kt, tok, tok, tok, tok, kmt, vm, bias)


NB_ROWS_PER_TILE = 4
NB_TQ = NB_ROWS_PER_TILE * GRID_W
NB_TILES_PER_STEP = 4
NB_LOOKAHEAD = 2


def _nbr_bias_table(rpb):
    h, n_dr, n_dc = rpb.shape
    w = GRID_W
    dc = np.arange(w)[None, :] - np.arange(w)[:, None] + NA_COLS - 1
    onehot = (dc[None] == np.arange(n_dc)[:, None, None]).astype(np.float32)
    blocks = jnp.einsum("hrd,dqk->hrqk", rpb.astype(F32), jnp.asarray(onehot),
                        precision=lax.Precision.HIGHEST)
    zero = jnp.zeros((h, w, w), F32)
    rows = []
    for a in range(NB_ROWS_PER_TILE):
        strip = []
        for b in range(3 * NB_ROWS_PER_TILE):
            dr = (b - NB_ROWS_PER_TILE) - a + (NA_ROWS - 1)
            strip.append(blocks[:, dr] if 0 <= dr < n_dr else zero)
        rows.append(jnp.concatenate(strip, axis=-1))
    return jnp.concatenate(rows, axis=1)


def _nbr_mask_table(n_tiles, n_rows):
    qi = np.arange(NB_TQ)[:, None]
    ki = np.arange(3 * NB_TQ)[None, :]
    c0 = np.clip(qi % GRID_W - NA_COLS // 2, 0, GRID_W - NA_COLS)
    k_col = ki % GRID_W
    col_ok = (k_col >= c0) & (k_col < c0 + NA_COLS)
    out = []
    for i in (0, min(1, n_tiles - 1), n_tiles - 1):
        q_row = i * NB_ROWS_PER_TILE + qi // GRID_W
        k_row = (i - 1) * NB_ROWS_PER_TILE + ki // GRID_W
        r0 = np.clip(q_row - NA_ROWS // 2, 0, n_rows - NA_ROWS)
        row_ok = (k_row >= r0) & (k_row < r0 + NA_ROWS)
        out.append(np.where(row_ok & col_ok, np.float32(0.0), np.float32(NEG)))
    return np.stack(out).astype(np.float32)


_NBR_CANONICAL_TILES = 3


def _nbr_logit_table(rpb):
    mask = _nbr_mask_table(_NBR_CANONICAL_TILES, _NBR_CANONICAL_TILES * NB_ROWS_PER_TILE)
    table = (_nbr_bias_table(rpb)[None] + jnp.asarray(mask)[:, None]) * LOG2E
    return table.reshape(3 * N_Q_HEADS, NB_TQ, 3 * NB_TQ)


def _nbr_kernel(q_ref, ktp_ref, ktc_ref, ktn_ref, vp_ref, vc_ref, vn_ref, qm_ref, kmt_ref, vm_ref, table_ref,
                o_ref, *, n_tiles):
    i = pl.program_id(1)
    tq = NB_TQ
    per = NB_TILES_PER_STEP
    low = _low_lanes(tq)
    kt_tiles = ([lambda cols: ktp_ref[0, cols, :]]
                + [(lambda cols, s=s: ktc_ref[0, cols, s * tq:(s + 1) * tq]) for s in range(per)]
                + [lambda cols: ktn_ref[0, cols, :]])
    v_tiles = ([lambda cols: vp_ref[0, :, cols]]
               + [(lambda cols, s=s: vc_ref[0, s * tq:(s + 1) * tq, cols]) for s in range(per)]
               + [lambda cols: vn_ref[0, :, cols]])
    def scores(s, p, hi):
        tile = i * per + s
        variant = jnp.where(tile == 0, 0, jnp.where(tile == n_tiles - 1, 2, 1))
        cols = slice(p * LANES, (p + 1) * LANES)
        qs = _split_heads(q_ref[0, s * tq:(s + 1) * tq, cols], low)[hi]
        kt = jnp.concatenate([f(cols) for f in kt_tiles[s:s + 3]], axis=1)
        sc = jnp.dot(qs, kt, preferred_element_type=F32)
        return sc + table_ref[variant * N_Q_HEADS + 2 * p + hi]

    pending = {}

    def finish(s, p, hi, sc):
        cols = slice(p * LANES, (p + 1) * LANES)
        pr = jnp.exp2(sc - jnp.max(sc, axis=-1, keepdims=True)).astype(BF16)
        v = jnp.concatenate([f(cols) for f in v_tiles[s:s + 3]], axis=0)
        oa = jnp.dot(pr, _with_ones(v), preferred_element_type=F32)
        if not hi:
            pending[(s, p)] = oa
            return
        both = jnp.concatenate([pending.pop((s, p)), oa], axis=0)
        o_ref[0, s * tq:(s + 1) * tq, cols] = (
            _merge_heads(both, low, slice(0, tq), slice(tq, 2 * tq)).astype(o_ref.dtype))

    stages = [(functools.partial(scores, s, p, hi), functools.partial(finish, s, p, hi))
              for s in range(per) for p in range(Q_WIDTH // LANES) for hi in range(2)]
    _run_pipelined(stages + _memory_stages(qm_ref, kmt_ref, vm_ref, o_ref, tq), NB_LOOKAHEAD)


def _nbr_attention(tok, kt, kmt, vm, table):
    b, t, _ = tok.shape
    tq = NB_TQ
    n = t // tq
    n_rows = t // GRID_W
    per = NB_TILES_PER_STEP
    ts = per * tq
    assert n_rows >= NA_ROWS and t % ts == 0 and n >= 2
    canonical = _nbr_mask_table(_NBR_CANONICAL_TILES, _NBR_CANONICAL_TILES * NB_ROWS_PER_TILE)
    assert n == 2 or np.array_equal(_nbr_mask_table(n, n_rows), canonical)
    assert n > 2 or np.array_equal(_nbr_mask_table(n, n_rows)[[0, 2]], canonical[[0, 2]])
    prev_i = lambda i: jnp.maximum(i * per - 1, 0)
    next_i = lambda i: jnp.minimum((i + 1) * per, n - 1)
    return pl.pallas_call(
        functools.partial(_nbr_kernel, n_tiles=n),
        grid=(b, t // ts),
        in_specs=[
            pl.BlockSpec((1, ts, Q_WIDTH), lambda bi, i: (bi, i, 0)),
            pl.BlockSpec((1, Q_WIDTH, tq), lambda bi, i: (bi, 0, prev_i(i))),
            pl.BlockSpec((1, Q_WIDTH, ts), lambda bi, i: (bi, 0, i)),
            pl.BlockSpec((1, Q_WIDTH, tq), lambda bi, i: (bi, 0, next_i(i))),
            pl.BlockSpec((1, tq, Q_WIDTH), lambda bi, i: (bi, prev_i(i), 1)),
            pl.BlockSpec((1, ts, Q_WIDTH), lambda bi, i: (bi, i, 1)),
            pl.BlockSpec((1, tq, Q_WIDTH), lambda bi, i: (bi, next_i(i), 1)),
            pl.BlockSpec((1, ts, MEM_WIDTH), lambda bi, i: (bi, i, 2 * Q_WIDTH // MEM_WIDTH)),
            pl.BlockSpec((1, MEM_WIDTH, MEM_LEN), lambda bi, i: (bi, 0, 0)),
            pl.BlockSpec((1, MEM_LEN, MEM_WIDTH), lambda bi, i: (bi, 0, 0)),
            _resident(table.shape, lambda bi, i: (0, 0, 0)),
        ],
        out_specs=pl.BlockSpec((1, ts, MIX_WIDTH), lambda bi, i: (bi, i, 0)),
        out_shape=jax.ShapeDtypeStruct((b, t, MIX_WIDTH), BF16),
        compiler_params=_params("parallel", "parallel"),
        name="nbr_attention",
    )(tok, kt, kt, kt, tok, tok, tok, tok, kmt, vm, table)


def _post_kernel(xp_ref, xc_ref, xn_ref, op_ref, oc_ref, on_ref, wo_ref, gn_ref, wg_ref, wu_ref, cw_ref, cb_ref,
                 wd_ref, gf_ref, y_ref, oext_ref, hext_ref, act_ref, *, final_norm):
    i = pl.program_id(1)
    n = pl.num_programs(1)
    tm = xc_ref.shape[1]
    ext = tm + HALO
    halo_row = lax.broadcasted_iota(jnp.int32, (HALO, 1), 0)
    take_next = halo_row < HALO // 2
    oext_ref[0:tm] = oc_ref[0]
    oext_ref[tm:] = jnp.where(take_next, on_ref[0], op_ref[0])
    mix = jnp.dot(oext_ref[...], wo_ref[...], preferred_element_type=F32)
    gn = gn_ref[...]
    x1 = xc_ref[0] + mix[0:tm]
    y_ref[0] = x1
    hext_ref[0:tm] = _rms(x1, gn).astype(BF16)
    h_halo = _rms(jnp.where(take_next, xn_ref[0], xp_ref[0]) + mix[tm:], gn).astype(BF16)
    halo_ok = ((halo_row == 0) & (i < n - 1)) | ((halo_row == HALO - 1) & (i > 0))
    hext_ref[tm:] = jnp.where(halo_ok, h_halo, jnp.zeros_like(h_halo))

    for lo, hi in zip(FF_CHUNK_BOUNDS[:-1], FF_CHUNK_BOUNDS[1:]):
        cols = slice(lo, hi)
        g_ext = jnp.dot(hext_ref[...], wg_ref[:, cols], preferred_element_type=F32)
        g = g_ext[0:tm]
        g_prev = pltpu.roll(g_ext, 1, 0)[0:tm]
        g_next = pltpu.roll(g_ext, ext - 1, 0)[0:tm]
        gc = g_prev * cw_ref[0:1, cols] + g * cw_ref[1:2, cols] + g_next * cw_ref[2:3, cols] + cb_ref[:, cols]
        u = jnp.dot(hext_ref[0:tm], wu_ref[:, cols], preferred_element_type=F32)
        act_ref[:, cols] = (gc * jax.nn.sigmoid(gc) * u).astype(BF16)

    y = y_ref[0] + jnp.dot(act_ref[...], wd_ref[...], preferred_element_type=F32)
    if final_norm:
        y = _rms(y, gf_ref[...])
    y_ref[0] = y


def _post_attention(x, o, w_o, ffn, layer, g_final, tm, final_norm):
    b, t, d = x.shape
    per = tm // HALO
    n_halo = t // HALO
    kernel = functools.partial(_post_kernel, final_norm=final_norm)
    const2 = lambda bi, i: (0, 0)
    of_layer = lambda a: pl.BlockSpec((None,) + a.shape[1:], lambda bi, i: (layer, 0, 0),
                                      pipeline_mode=pl.Buffered(1))
    prev_map = lambda bi, i: (bi, jnp.maximum(i * per - 1, 0), 0)
    cur_map = lambda bi, i: (bi, i, 0)
    next_map = lambda bi, i: (bi, jnp.minimum((i + 1) * per, n_halo - 1), 0)
    return pl.pallas_call(
        kernel,
        grid=(b, t // tm),
        in_specs=[
            pl.BlockSpec((1, HALO, d), prev_map),
            pl.BlockSpec((1, tm, d), cur_map),
            pl.BlockSpec((1, HALO, d), next_map),
            pl.BlockSpec((1, HALO, MIX_WIDTH), prev_map),
            pl.BlockSpec((1, tm, MIX_WIDTH), cur_map),
            pl.BlockSpec((1, HALO, MIX_WIDTH), next_map),
            _resident(w_o.shape, const2),
            of_layer(ffn["g_ffn"]),
            of_layer(ffn["w_gate"]),
            of_layer(ffn["w_up"]),
            of_layer(ffn["conv_w"]),
            of_layer(ffn["conv_b"]),
            of_layer(ffn["w_down"]),
            _resident((1, d), const2),
        ],
        out_specs=pl.BlockSpec((1, tm, d), cur_map),
        out_shape=jax.ShapeDtypeStruct((b, t, d), F32),
        scratch_shapes=[pltpu.VMEM((tm + HALO, MIX_WIDTH), BF16), pltpu.VMEM((tm + HALO, d), BF16),
                        pltpu.VMEM((tm, D_FF), BF16)],
        compiler_params=_params("parallel", "parallel"),
        name="post_attention",
    )(x, x, x, o, o, o, w_o, ffn["g_ffn"], ffn["w_gate"], ffn["w_up"], ffn["conv_w"], ffn["conv_b"], ffn["w_down"],
      g_final.reshape(1, d))


def _prepare_weights(g_mix, g_mem, w_in_a, sink_a, w_in_b, rpb_b, w_mem_kv, w_o, g_ffn, w_gate, w_up, conv_w,
                     conv_b, w_down, g_final):
    head_cols = np.arange(Q_WIDTH).reshape(N_Q_HEADS, HEAD_DIM)
    perm_a = head_cols[_window_head_order()].reshape(-1)
    layers = []
    for i in range(DEPTH):
        j = i // 2
        wo = w_o[i]
        if i % 2 == 0:
            w = w_in_a[j]
            wq, wk, wv, wqm = (w[:, :Q_WIDTH], w[:, Q_WIDTH:Q_WIDTH + KV_WIDTH_A],
                               w[:, Q_WIDTH + KV_WIDTH_A:Q_WIDTH + 2 * KV_WIDTH_A], w[:, Q_WIDTH + 2 * KV_WIDTH_A:])
            wq = wq[:, perm_a]
            wo = jnp.concatenate([wo[:Q_WIDTH][perm_a], wo[Q_WIDTH:]], axis=0)
            extra = dict(sink=sink_a[j].astype(F32) * LOG2E, bias=jnp.asarray(_window_bias_table()))
        else:
            w = w_in_b[j]
            wq, wk, wv, wqm = (w[:, :Q_WIDTH], w[:, Q_WIDTH:2 * Q_WIDTH], w[:, 2 * Q_WIDTH:3 * Q_WIDTH],
                               w[:, 3 * Q_WIDTH:])
            extra = dict(bias=_nbr_logit_table(rpb_b[j]))
        tok_scale = np.concatenate([np.full(wq.shape[1], Q_SCALE), np.ones(wv.shape[1]),
                                    np.full(wqm.shape[1], Q_SCALE)]).astype(np.float32)
        layers.append(dict(
            g_mix=g_mix[i],
            w_tok=jnp.concatenate([wq, wv, wqm], axis=1).astype(BF16), tok_scale=jnp.asarray(tok_scale),
            w_kt=wk.T.astype(BF16),
            w_o=wo.astype(BF16), **extra))
    ffn = dict(g_ffn=g_ffn.reshape(DEPTH, 1, D_MODEL), w_gate=w_gate.astype(BF16), w_up=w_up.astype(BF16),
               conv_w=conv_w, conv_b=conv_b.reshape(DEPTH, 1, D_FF), w_down=w_down.astype(BF16))
    mem_weights = dict(g_mem=g_mem, w_vm=w_mem_kv[:, :, MEM_WIDTH:].astype(BF16),
                       w_kmt=jnp.swapaxes(w_mem_kv[:, :, :MEM_WIDTH], 1, 2).astype(BF16))
    return layers, ffn, mem_weights


def _trunk(x, mem, layers, ffn, mem_weights, g_final):
    b, t, d = x.shape
    n = b * t
    tm = 512
    mem_kv = _memory_project(mem, mem_weights["g_mem"], mem_weights["w_vm"], mem_weights["w_kmt"])
    for i, lw in enumerate(layers):
        tok, kt = _norm_project(x.reshape(n, d), lw["g_mix"], lw["w_tok"], lw["tok_scale"], lw["w_kt"], b, 2 * tm)
        tok = tok.reshape(b, t, -1)
        vm, kmt = mem_kv[i]
        if i % 2 == 0:
            o = _window_attention(tok, kt, kmt, vm, lw["sink"], lw["bias"], tq=1024)
        else:
            o = _nbr_attention(tok, kt, kmt, vm, lw["bias"])
        x = _post_attention(x, o, lw["w_o"], ffn, i, g_final, 2 * tm, final_norm=(i == DEPTH - 1))
    return x


def kernel(x_prompt, x_sample, mem_prompt, mem_sample, g_mix, g_mem, w_in_a, sink_a, w_in_b, rpb_b, w_mem_kv, w_o,
           g_ffn, w_gate, w_up, conv_w, conv_b, w_down, g_final):
    layers, ffn, mem_weights = _prepare_weights(g_mix, g_mem, w_in_a, sink_a, w_in_b, rpb_b, w_mem_kv, w_o, g_ffn,
                                                w_gate, w_up, conv_w, conv_b, w_down, g_final)
    y_prompt = _trunk(x_prompt, mem_prompt, layers, ffn, mem_weights, g_final)
    y_sample = _trunk(x_sample, mem_sample, layers, ffn, mem_weights, g_final)
    return (y_prompt, y_sample)
```

```python
import functools

import numpy as np
import jax
import jax.numpy as jnp
from jax import lax
from jax.experimental import pallas as pl
from jax.experimental.pallas import tpu as pltpu

D_MODEL = 1024
DEPTH = 2
HEAD_DIM = 64
N_Q_HEADS = 12
N_KV_HEADS_A = 4
N_MEM_HEADS = 4
MEM_LEN = 256
WINDOW = 128
GRID_W = 64
NA_ROWS = 8
NA_COLS = 16
D_FF = 2816
CONV_W = 3
EPS = 1e-6
NEG = -1e30
Q_WIDTH = N_Q_HEADS * HEAD_DIM
KV_WIDTH_A = N_KV_HEADS_A * HEAD_DIM
MEM_WIDTH = N_MEM_HEADS * HEAD_DIM
MIX_WIDTH = Q_WIDTH + MEM_WIDTH
LOG2E = 1.4426950408889634
Q_SCALE = HEAD_DIM ** -0.5 * LOG2E
REP_A = N_Q_HEADS // N_KV_HEADS_A

F32 = jnp.float32
BF16 = jnp.bfloat16

LANES = 128
VMEM_LIMIT_BYTES = 60 * 1024 * 1024
MXU_WIDTH = 256
FF_CHUNK_BOUNDS = (0, 6 * MXU_WIDTH, D_FF)
HALO = 16

PROJ_ROWS = 1024
WINDOW_ROWS = 1024
POST_ROWS = 1024

_NT = (((1,), (1,)), ((), ()))


def _params(*sem):
    return pltpu.CompilerParams(dimension_semantics=sem, vmem_limit_bytes=VMEM_LIMIT_BYTES)


def _resident(shape, index_map):
    return pl.BlockSpec(shape, index_map, pipeline_mode=pl.Buffered(1))


def _rms(x, g):
    return x * lax.rsqrt(jnp.mean(x * x, axis=-1, keepdims=True) + EPS) * g


PROJ_PARTS = 2


def _proj_kernel(x_ref, g_ref, w_ref, cs_ref, wt_ref, tok_ref, feat_ref):
    part = x_ref.shape[0] // PROJ_PARTS
    for r in range(PROJ_PARTS):
        rows = slice(r * part, (r + 1) * part)
        h = _rms(x_ref[rows, :], g_ref[...]).astype(BF16)
        tok = jnp.dot(h, w_ref[...], preferred_element_type=F32) * cs_ref[...]
        tok_ref[rows, :] = tok.astype(tok_ref.dtype)
        feat_ref[0, :, rows] = lax.dot_general(wt_ref[...], h, _NT, preferred_element_type=F32).astype(feat_ref.dtype)


def _norm_project(x, g, w_tok, col_scale, w_feat_t, batch, tm):
    n, d = x.shape
    t = n // batch
    per_seq = t // tm
    wdt = w_tok.shape[1]
    wft = w_feat_t.shape[0]
    return pl.pallas_call(
        _proj_kernel,
        grid=(n // tm,),
        in_specs=[
            pl.BlockSpec((tm, d), lambda i: (i, 0)),
            _resident((1, d), lambda i: (0, 0)),
            _resident((d, wdt), lambda i: (0, 0)),
            _resident((1, wdt), lambda i: (0, 0)),
            _resident((wft, d), lambda i: (0, 0)),
        ],
        out_specs=[pl.BlockSpec((tm, wdt), lambda i: (i, 0)),
                   pl.BlockSpec((1, wft, tm), lambda i: (i // per_seq, 0, i % per_seq))],
        out_shape=[jax.ShapeDtypeStruct((n, wdt), BF16), jax.ShapeDtypeStruct((batch, wft, t), BF16)],
        compiler_params=_params("parallel"),
        name="norm_project",
    )(x, g.reshape(1, d), w_tok, col_scale.reshape(1, wdt), w_feat_t)


def _mem_proj_kernel(m_ref, g_ref, wv_ref, wkt_ref, *out_refs):
    x = m_ref[...]
    y = x * lax.rsqrt(jnp.mean(x * x, axis=-1, keepdims=True) + EPS)
    n_seq = m_ref.shape[0] // MEM_LEN
    for l in range(DEPTH):
        vm_ref, kmt_ref = out_refs[2 * l], out_refs[2 * l + 1]
        h = (y * g_ref[l:l + 1, :]).astype(BF16)
        vm_ref[...] = jnp.dot(h, wv_ref[l], preferred_element_type=F32).astype(vm_ref.dtype)
        for s in range(n_seq):
            kmt_ref[s] = lax.dot_general(wkt_ref[l], h[s * MEM_LEN:(s + 1) * MEM_LEN], _NT,
                                         preferred_element_type=F32).astype(kmt_ref.dtype)


def _memory_project(mem, g_mem, w_vm, w_kmt):
    b, _, d = mem.shape
    n_seq = next(c for c in (4, 2, 1) if b % c == 0)
    tm = n_seq * MEM_LEN
    outs = pl.pallas_call(
        _mem_proj_kernel,
        grid=(b // n_seq,),
        in_specs=[
            pl.BlockSpec((tm, d), lambda i: (i, 0)),
            _resident(g_mem.shape, lambda i: (0, 0)),
            _resident(w_vm.shape, lambda i: (0, 0, 0)),
            _resident(w_kmt.shape, lambda i: (0, 0, 0)),
        ],
        out_specs=[pl.BlockSpec((tm, MEM_WIDTH), lambda i: (i, 0)),
                   pl.BlockSpec((n_seq, MEM_WIDTH, MEM_LEN), lambda i: (i, 0, 0))] * DEPTH,
        out_shape=[jax.ShapeDtypeStruct((b * MEM_LEN, MEM_WIDTH), BF16),
                   jax.ShapeDtypeStruct((b, MEM_WIDTH, MEM_LEN), BF16)] * DEPTH,
        compiler_params=_params("parallel"),
        name="memory_project",
    )(mem.reshape(b * MEM_LEN, d), g_mem, w_vm, w_kmt)
    return [(outs[2 * l].reshape(b, MEM_LEN, MEM_WIDTH), outs[2 * l + 1]) for l in range(DEPTH)]


def _low_lanes(rows):
    return lax.broadcasted_iota(jnp.int32, (rows, LANES), 1) < HEAD_DIM


def _split_heads(q_tile, low):
    zero = jnp.zeros_like(q_tile)
    return jnp.where(low, q_tile, zero), jnp.where(low, zero, q_tile)


def _with_ones(v):
    return jnp.concatenate([v, jnp.ones_like(v)], axis=1)


def _merge_heads(oa, low, lo_rows, hi_rows, extra_den=None):
    pick = lambda a: jnp.where(low, a[lo_rows], a[hi_rows])
    den = pick(oa[:, LANES:])
    if extra_den is not None:
        den = den + pick(extra_den)
    return pick(oa[:, :LANES]) / den


def _run_pipelined(stages, ahead):
    pending = [stages[k][0]() for k in range(min(ahead, len(stages)))]
    for k, (_, finish) in enumerate(stages):
        if k + ahead < len(stages):
            pending.append(stages[k + ahead][0]())
        finish(pending.pop(0))


def _memory_stages(qm_ref, kmt_ref, vm_ref, o_ref, tq):
    low = _low_lanes(tq)
    stages = []
    for r in range(qm_ref.shape[1] // tq):
        rows = slice(r * tq, (r + 1) * tq)
        for p in range(MEM_WIDTH // LANES):
            cols = slice(p * LANES, (p + 1) * LANES)

            def scores(rows=rows, cols=cols):
                qs = jnp.concatenate(_split_heads(qm_ref[0, rows, cols], low), axis=0)
                return jnp.dot(qs, kmt_ref[0, cols, :], preferred_element_type=F32)

            def finish(s, rows=rows, cols=cols, p=p):
                pr = jnp.exp2(s - jnp.max(s, axis=-1, keepdims=True)).astype(BF16)
                oa = jnp.dot(pr, _with_ones(vm_ref[0, :, cols]), preferred_element_type=F32)
                o_ref[0, rows, Q_WIDTH + p * LANES:Q_WIDTH + (p + 1) * LANES] = (
                    _merge_heads(oa, low, slice(0, tq), slice(tq, 2 * tq)).astype(o_ref.dtype))

            stages.append((scores, finish))
    return stages


WINDOW_TILES_PER_STAGE = 1
WINDOW_LOOKAHEAD = 3


def _window_head(pair, row_block):
    return REP_A * (2 * pair + row_block % 2) + row_block // 2


def _window_head_order():
    return [_window_head(p, rb) for p in range(N_KV_HEADS_A // 2) for rb in range(2 * REP_A)]


def _window_bias_table():
    slopes = (2.0 ** (-8.0 * np.arange(1, N_Q_HEADS + 1, dtype=np.float32) / N_Q_HEADS)).astype(np.float32)
    blk = WINDOW
    qq = np.arange(blk)[:, None]
    kk = np.arange(3 * blk)[None, :]
    arel = np.abs(qq - kk + WINDOW)
    in_window = arel <= WINDOW
    in_seq = [kk >= blk, np.ones_like(kk, bool), kk < 2 * blk]
    out = np.empty((3, N_KV_HEADS_A // 2, 2 * REP_A, blk, 3 * blk), np.float32)
    for var in range(3):
        for p in range(N_KV_HEADS_A // 2):
            for rb in range(2 * REP_A):
                alibi = -slopes[_window_head(p, rb)] * arel.astype(np.float32) * np.float32(LOG2E)
                out[var, p, rb] = np.where(in_window & in_seq[var], alibi, np.float32(NEG))
    return out.reshape(3 * (N_KV_HEADS_A // 2), 2 * REP_A * blk, 3 * blk)


def _window_kernel(sink_ref, q_ref, ktp_ref, ktc_ref, ktn_ref, vp_ref, vc_ref, vn_ref, qm_ref, kmt_ref, vm_ref,
                   bias_ref, o_ref, ktbuf, vbuf, *, tq, n_blk):
    i = pl.program_id(1)
    blk = WINDOW
    n_pair = N_KV_HEADS_A // 2
    ktbuf[:, 0:blk] = ktp_ref[0]
    ktbuf[:, blk:blk + tq] = ktc_ref[0]
    ktbuf[:, blk + tq:] = ktn_ref[0]
    vbuf[0:blk] = vp_ref[0]
    vbuf[blk:blk + tq] = vc_ref[0]
    vbuf[blk + tq:] = vn_ref[0]

    low = _low_lanes(blk)
    sinks = [jnp.concatenate([jnp.full((blk, LANES), sink_ref[_window_head(p, rb)], F32)
                              for rb in range(2 * REP_A)], axis=0) for p in range(n_pair)]
    nt = WINDOW_TILES_PER_STAGE

    def scores(j, p, t0):
        g_blk = i * (tq // blk) + j
        variant = jnp.where(g_blk == 0, 0, jnp.where(g_blk == n_blk - 1, 2, 1))
        tiles = []
        for t in range(t0, t0 + nt):
            c0 = (REP_A * p + t) * LANES
            tiles.extend(_split_heads(q_ref[0, j * blk:(j + 1) * blk, c0:c0 + LANES], low))
        qs = jnp.concatenate(tiles, axis=0)
        sc = jnp.dot(qs, ktbuf[p * LANES:(p + 1) * LANES, j * blk:(j + 3) * blk], preferred_element_type=F32)
        return sc + bias_ref[variant * n_pair + p, 2 * t0 * blk:2 * (t0 + nt) * blk, :]

    def finish(j, p, t0, sc):
        sink = sinks[p][2 * t0 * blk:2 * (t0 + nt) * blk]
        m = jnp.maximum(jnp.max(sc, axis=-1, keepdims=True), sink)
        pr = jnp.concatenate([jnp.exp2(sc[:, c * LANES:(c + 1) * LANES] - m) for c in range(3 * blk // LANES)],
                             axis=1).astype(BF16)
        oa = jnp.dot(pr, _with_ones(vbuf[j * blk:(j + 3) * blk, p * LANES:(p + 1) * LANES]),
                     preferred_element_type=F32)
        sink_mass = jnp.exp2(sink - m)
        for k in range(nt):
            c0 = (REP_A * p + t0 + k) * LANES
            o_ref[0, j * blk:(j + 1) * blk, c0:c0 + LANES] = _merge_heads(
                oa, low, slice(2 * k * blk, (2 * k + 1) * blk), slice((2 * k + 1) * blk, (2 * k + 2) * blk),
                extra_den=sink_mass).astype(o_ref.dtype)

    stages = [(functools.partial(scores, j, p, t0), functools.partial(finish, j, p, t0))
              for j in range(tq // blk) for p in range(n_pair) for t0 in range(0, REP_A, nt)]
    _run_pipelined(stages + _memory_stages(qm_ref, kmt_ref, vm_ref, o_ref, 2 * blk), WINDOW_LOOKAHEAD)


def _window_attention(tok, kt, kmt, vm, sink, bias, tq):
    b, t, _ = tok.shape
    blk = WINDOW
    n_blk = t // blk
    per = tq // blk
    assert n_blk >= 2 and t % tq == 0
    vcol = Q_WIDTH // KV_WIDTH_A
    prev_i = lambda i: jnp.maximum(i * per - 1, 0)
    next_i = lambda i: jnp.minimum((i + 1) * per, n_blk - 1)
    kernel = functools.partial(_window_kernel, tq=tq, n_blk=n_blk)
    return pl.pallas_call(
        kernel,
        grid=(b, t // tq),
        in_specs=[
            pl.BlockSpec(memory_space=pltpu.SMEM),
            pl.BlockSpec((1, tq, Q_WIDTH), lambda bi, i: (bi, i, 0)),
            pl.BlockSpec((1, KV_WIDTH_A, blk), lambda bi, i: (bi, 0, prev_i(i))),
            pl.BlockSpec((1, KV_WIDTH_A, tq), lambda bi, i: (bi, 0, i)),
            pl.BlockSpec((1, KV_WIDTH_A, blk), lambda bi, i: (bi, 0, next_i(i))),
            pl.BlockSpec((1, blk, KV_WIDTH_A), lambda bi, i: (bi, prev_i(i), vcol)),
            pl.BlockSpec((1, tq, KV_WIDTH_A), lambda bi, i: (bi, i, vcol)),
            pl.BlockSpec((1, blk, KV_WIDTH_A), lambda bi, i: (bi, next_i(i), vcol)),
            pl.BlockSpec((1, tq, MEM_WIDTH), lambda bi, i: (bi, i, vcol + 1)),
            pl.BlockSpec((1, MEM_WIDTH, MEM_LEN), lambda bi, i: (bi, 0, 0)),
            pl.BlockSpec((1, MEM_LEN, MEM_WIDTH), lambda bi, i: (bi, 0, 0)),
            _resident(bias.shape, lambda bi, i: (0, 0, 0)),
        ],
        out_specs=pl.BlockSpec((1, tq, MIX_WIDTH), lambda bi, i: (bi, i, 0)),
        out_shape=jax.ShapeDtypeStruct((b, t, MIX_WIDTH), BF16),
        scratch_shapes=[pltpu.VMEM((KV_WIDTH_A, tq + 2 * blk), BF16),
                        pltpu.VMEM((tq + 2 * blk, KV_WIDTH_A), BF16)],
        compiler_params=_params("parallel", "parallel"),
        name="window_attention",
    )(sink, tok, kt, kt, kt, tok, tok, tok, tok, kmt, vm, bias)


NB_ROWS_PER_TILE = 4
NB_TQ = NB_ROWS_PER_TILE * GRID_W
NB_TILES_PER_STEP = 4
NB_LOOKAHEAD = 2


def _nbr_bias_table(rpb):
    h, n_dr, n_dc = rpb.shape
    w = GRID_W
    dc = np.arange(w)[None, :] - np.arange(w)[:, None] + NA_COLS - 1
    onehot = (dc[None] == np.arange(n_dc)[:, None, None]).astype(np.float32)
    blocks = jnp.einsum("hrd,dqk->hrqk", rpb.astype(F32), jnp.asarray(onehot),
                        precision=lax.Precision.HIGHEST)
    zero = jnp.zeros((h, w, w), F32)
    rows = []
    for a in range(NB_ROWS_PER_TILE):
        strip = []
        for b in range(3 * NB_ROWS_PER_TILE):
            dr = (b - NB_ROWS_PER_TILE) - a + (NA_ROWS - 1)
            strip.append(blocks[:, dr] if 0 <= dr < n_dr else zero)
        rows.append(jnp.concatenate(strip, axis=-1))
    return jnp.concatenate(rows, axis=1)


def _nbr_mask_table(n_tiles, n_rows):
    qi = np.arange(NB_TQ)[:, None]
    ki = np.arange(3 * NB_TQ)[None, :]
    c0 = np.clip(qi % GRID_W - NA_COLS // 2, 0, GRID_W - NA_COLS)
    k_col = ki % GRID_W
    col_ok = (k_col >= c0) & (k_col < c0 + NA_COLS)
    out = []
    for i in (0, min(1, n_tiles - 1), n_tiles - 1):
        q_row = i * NB_ROWS_PER_TILE + qi // GRID_W
        k_row = (i - 1) * NB_ROWS_PER_TILE + ki // GRID_W
        r0 = np.clip(q_row - NA_ROWS // 2, 0, n_rows - NA_ROWS)
        row_ok = (k_row >= r0) & (k_row < r0 + NA_ROWS)
        out.append(np.where(row_ok & col_ok, np.float32(0.0), np.float32(NEG)))
    return np.stack(out).astype(np.float32)


_NBR_CANONICAL_TILES = 3


def _nbr_logit_table(rpb):
    mask = _nbr_mask_table(_NBR_CANONICAL_TILES, _NBR_CANONICAL_TILES * NB_ROWS_PER_TILE)
    table = (_nbr_bias_table(rpb)[None] + jnp.asarray(mask)[:, None]) * LOG2E
    return table.reshape(3 * N_Q_HEADS, NB_TQ, 3 * NB_TQ)


def _nbr_kernel(q_ref, ktp_ref, ktc_ref, ktn_ref, vp_ref, vc_ref, vn_ref, qm_ref, kmt_ref, vm_ref, table_ref,
                o_ref, *, n_tiles):
    i = pl.program_id(1)
    tq = NB_TQ
    per = NB_TILES_PER_STEP
    low = _low_lanes(tq)
    kt_tiles = ([lambda cols: ktp_ref[0, cols, :]]
                + [(lambda cols, s=s: ktc_ref[0, cols, s * tq:(s + 1) * tq]) for s in range(per)]
                + [lambda cols: ktn_ref[0, cols, :]])
    v_tiles = ([lambda cols: vp_ref[0, :, cols]]
               + [(lambda cols, s=s: vc_ref[0, s * tq:(s + 1) * tq, cols]) for s in range(per)]
               + [lambda cols: vn_ref[0, :, cols]])
    def scores(s, p, hi):
        tile = i * per + s
        variant = jnp.where(tile == 0, 0, jnp.where(tile == n_tiles - 1, 2, 1))
        cols = slice(p * LANES, (p + 1) * LANES)
        qs = _split_heads(q_ref[0, s * tq:(s + 1) * tq, cols], low)[hi]
        kt = jnp.concatenate([f(cols) for f in kt_tiles[s:s + 3]], axis=1)
        sc = jnp.dot(qs, kt, preferred_element_type=F32)
        return sc + table_ref[variant * N_Q_HEADS + 2 * p + hi]

    pending = {}

    def finish(s, p, hi, sc):
        cols = slice(p * LANES, (p + 1) * LANES)
        pr = jnp.exp2(sc - jnp.max(sc, axis=-1, keepdims=True)).astype(BF16)
        v = jnp.concatenate([f(cols) for f in v_tiles[s:s + 3]], axis=0)
        oa = jnp.dot(pr, _with_ones(v), preferred_element_type=F32)
        if not hi:
            pending[(s, p)] = oa
            return
        both = jnp.concatenate([pending.pop((s, p)), oa], axis=0)
        o_ref[0, s * tq:(s + 1) * tq, cols] = (
            _merge_heads(both, low, slice(0, tq), slice(tq, 2 * tq)).astype(o_ref.dtype))

    stages = [(functools.partial(scores, s, p, hi), functools.partial(finish, s, p, hi))
              for s in range(per) for p in range(Q_WIDTH // LANES) for hi in range(2)]
    _run_pipelined(stages + _memory_stages(qm_ref, kmt_ref, vm_ref, o_ref, tq), NB_LOOKAHEAD)


def _nbr_attention(tok, kt, kmt, vm, table):
    b, t, _ = tok.shape
    tq = NB_TQ
    n = t // tq
    n_rows = t // GRID_W
    per = NB_TILES_PER_STEP
    ts = per * tq
    assert n_rows >= NA_ROWS and t % ts == 0 and n >= 2
    canonical = _nbr_mask_table(_NBR_CANONICAL_TILES, _NBR_CANONICAL_TILES * NB_ROWS_PER_TILE)
    assert n == 2 or np.array_equal(_nbr_mask_table(n, n_rows), canonical)
    assert n > 2 or np.array_equal(_nbr_mask_table(n, n_rows)[[0, 2]], canonical[[0, 2]])
    prev_i = lambda i: jnp.maximum(i * per - 1, 0)
    next_i = lambda i: jnp.minimum((i + 1) * per, n - 1)
    return pl.pallas_call(
        functools.partial(_nbr_kernel, n_tiles=n),
        grid=(b, t // ts),
        in_specs=[
            pl.BlockSpec((1, ts, Q_WIDTH), lambda bi, i: (bi, i, 0)),
            pl.BlockSpec((1, Q_WIDTH, tq), lambda bi, i: (bi, 0, prev_i(i))),
            pl.BlockSpec((1, Q_WIDTH, ts), lambda bi, i: (bi, 0, i)),
            pl.BlockSpec((1, Q_WIDTH, tq), lambda bi, i: (bi, 0, next_i(i))),
            pl.BlockSpec((1, tq, Q_WIDTH), lambda bi, i: (bi, prev_i(i), 1)),
            pl.BlockSpec((1, ts, Q_WIDTH), lambda bi, i: (bi, i, 1)),
            pl.BlockSpec((1, tq, Q_WIDTH), lambda bi, i: (bi, next_i(i), 1)),
            pl.BlockSpec((1, ts, MEM_WIDTH), lambda bi, i: (bi, i, 2 * Q_WIDTH // MEM_WIDTH)),
            pl.BlockSpec((1, MEM_WIDTH, MEM_LEN), lambda bi, i: (bi, 0, 0)),
            pl.BlockSpec((1, MEM_LEN, MEM_WIDTH), lambda bi, i: (bi, 0, 0)),
            _resident(table.shape, lambda bi, i: (0, 0, 0)),
        ],
        out_specs=pl.BlockSpec((1, ts, MIX_WIDTH), lambda bi, i: (bi, i, 0)),
        out_shape=jax.ShapeDtypeStruct((b, t, MIX_WIDTH), BF16),
        compiler_params=_params("parallel", "parallel"),
        name="nbr_attention",
    )(tok, kt, kt, kt, tok, tok, tok, tok, kmt, vm, table)


def _post_kernel(xp_ref, xc_ref, xn_ref, op_ref, oc_ref, on_ref, wo_ref, gn_ref, wg_ref, wu_ref, cw_ref, cb_ref,
                 wd_ref, gf_ref, y_ref, oext_ref, hext_ref, act_ref, *, final_norm):
    i = pl.program_id(1)
    n = pl.num_programs(1)
    tm = xc_ref.shape[1]
    ext = tm + HALO
    halo_row = lax.broadcasted_iota(jnp.int32, (HALO, 1), 0)
    take_next = halo_row < HALO // 2
    oext_ref[0:tm] = oc_ref[0]
    oext_ref[tm:] = jnp.where(take_next, on_ref[0], op_ref[0])
    mix = jnp.dot(oext_ref[...], wo_ref[...], preferred_element_type=F32)
    gn = gn_ref[...]
    x1 = xc_ref[0] + mix[0:tm]
    y_ref[0] = x1
    hext_ref[0:tm] = _rms(x1, gn).astype(BF16)
    h_halo = _rms(jnp.where(take_next, xn_ref[0], xp_ref[0]) + mix[tm:], gn).astype(BF16)
    halo_ok = ((halo_row == 0) & (i < n - 1)) | ((halo_row == HALO - 1) & (i > 0))
    hext_ref[tm:] = jnp.where(halo_ok, h_halo, jnp.zeros_like(h_halo))

    for lo, hi in zip(FF_CHUNK_BOUNDS[:-1], FF_CHUNK_BOUNDS[1:]):
        cols = slice(lo, hi)
        g_ext = jnp.dot(hext_ref[...], wg_ref[:, cols], preferred_element_type=F32)
        g = g_ext[0:tm]
        g_prev = pltpu.roll(g_ext, 1, 0)[0:tm]
        g_next = pltpu.roll(g_ext, ext - 1, 0)[0:tm]
        gc = g_prev * cw_ref[0:1, cols] + g * cw_ref[1:2, cols] + g_next * cw_ref[2:3, cols] + cb_ref[:, cols]
        u = jnp.dot(hext_ref[0:tm], wu_ref[:, cols], preferred_element_type=F32)
        act_ref[:, cols] = (gc * jax.nn.sigmoid(gc) * u).astype(BF16)

    y = y_ref[0] + jnp.dot(act_ref[...], wd_ref[...], preferred_element_type=F32)
    if final_norm:
        y = _rms(y, gf_ref[...])
    y_ref[0] = y


def _post_attention(x, o, w_o, ffn, layer, g_final, tm, final_norm):
    b, t, d = x.shape
    per = tm // HALO
    n_halo = t // HALO
    kernel = functools.partial(_post_kernel, final_norm=final_norm)
    const2 = lambda bi, i: (0, 0)
    of_layer = lambda a: pl.BlockSpec((None,) + a.shape[1:], lambda bi, i: (layer, 0, 0),
                                      pipeline_mode=pl.Buffered(1))
    prev_map = lambda bi, i: (bi, jnp.maximum(i * per - 1, 0), 0)
    cur_map = lambda bi, i: (bi, i, 0)
    next_map = lambda bi, i: (bi, jnp.minimum((i + 1) * per, n_halo - 1), 0)
    return pl.pallas_call(
        kernel,
        grid=(b, t // tm),
        in_specs=[
            pl.BlockSpec((1, HALO, d), prev_map),
            pl.BlockSpec((1, tm, d), cur_map),
            pl.BlockSpec((1, HALO, d), next_map),
            pl.BlockSpec((1, HALO, MIX_WIDTH), prev_map),
            pl.BlockSpec((1, tm, MIX_WIDTH), cur_map),
            pl.BlockSpec((1, HALO, MIX_WIDTH), next_map),
            _resident(w_o.shape, const2),
            of_layer(ffn["g_ffn"]),
            of_layer(ffn["w_gate"]),
            of_layer(ffn["w_up"]),
            of_layer(ffn["conv_w"]),
            of_layer(ffn["conv_b"]),
            of_layer(ffn["w_down"]),
            _resident((1, d), const2),
        ],
        out_specs=pl.BlockSpec((1, tm, d), cur_map),
        out_shape=jax.ShapeDtypeStruct((b, t, d), F32),
        scratch_shapes=[pltpu.VMEM((tm + HALO, MIX_WIDTH), BF16), pltpu.VMEM((tm + HALO, d), BF16),
                        pltpu.VMEM((tm, D_FF), BF16)],
        compiler_params=_params("parallel", "parallel"),
        name="post_attention",
    )(x, x, x, o, o, o, w_o, ffn["g_ffn"], ffn["w_gate"], ffn["w_up"], ffn["conv_w"], ffn["conv_b"], ffn["w_down"],
      g_final.reshape(1, d))


def _prepare_weights(g_mix, g_mem, w_in_a, sink_a, w_in_b, rpb_b, w_mem_kv, w_o, g_ffn, w_gate, w_up, conv_w,
                     conv_b, w_down, g_final):
    head_cols = np.arange(Q_WIDTH).reshape(N_Q_HEADS, HEAD_DIM)
    perm_a = head_cols[_window_head_order()].reshape(-1)
    layers = []
    for i in range(DEPTH):
        j = i // 2
        wo = w_o[i]
        if i % 2 == 0:
            w = w_in_a[j]
            wq, wk, wv, wqm = (w[:, :Q_WIDTH], w[:, Q_WIDTH:Q_WIDTH + KV_WIDTH_A],
                               w[:, Q_WIDTH + KV_WIDTH_A:Q_WIDTH + 2 * KV_WIDTH_A], w[:, Q_WIDTH + 2 * KV_WIDTH_A:])
            wq = wq[:, perm_a]
            wo = jnp.concatenate([wo[:Q_WIDTH][perm_a], wo[Q_WIDTH:]], axis=0)
            extra = dict(sink=sink_a[j].astype(F32) * LOG2E, bias=jnp.asarray(_window_bias_table()))
        else:
            w = w_in_b[j]
            wq, wk, wv, wqm = (w[:, :Q_WIDTH], w[:, Q_WIDTH:2 * Q_WIDTH], w[:, 2 * Q_WIDTH:3 * Q_WIDTH],
                               w[:, 3 * Q_WIDTH:])
            extra = dict(bias=_nbr_logit_table(rpb_b[j]))
        tok_scale = np.concatenate([np.full(wq.shape[1], Q_SCALE), np.ones(wv.shape[1]),
                                    np.full(wqm.shape[1], Q_SCALE)]).astype(np.float32)
        layers.append(dict(
            g_mix=g_mix[i],
            w_tok=jnp.concatenate([wq, wv, wqm], axis=1).astype(BF16), tok_scale=jnp.asarray(tok_scale),
            w_kt=wk.T.astype(BF16),
            w_o=wo.astype(BF16), **extra))
    ffn = dict(g_ffn=g_ffn.reshape(DEPTH, 1, D_MODEL), w_gate=w_gate.astype(BF16), w_up=w_up.astype(BF16),
               conv_w=conv_w, conv_b=conv_b.reshape(DEPTH, 1, D_FF), w_down=w_down.astype(BF16))
    mem_weights = dict(g_mem=g_mem, w_vm=w_mem_kv[:, :, MEM_WIDTH:].astype(BF16),
                       w_kmt=jnp.swapaxes(w_mem_kv[:, :, :MEM_WIDTH], 1, 2).astype(BF16))
    return layers, ffn, mem_weights


def _trunk(x, mem, layers, ffn, mem_weights, g_final):
    b, t, d = x.shape
    n = b * t
    assert t % PROJ_ROWS == 0 and t % WINDOW_ROWS == 0 and t % POST_ROWS == 0
    mem_kv = _memory_project(mem, mem_weights["g_mem"], mem_weights["w_vm"], mem_weights["w_kmt"])
    for i, lw in enumerate(layers):
        tok, kt = _norm_project(x.reshape(n, d), lw["g_mix"], lw["w_tok"], lw["tok_scale"], lw["w_kt"], b, PROJ_ROWS)
        tok = tok.reshape(b, t, -1)
        vm, kmt = mem_kv[i]
        if i % 2 == 0:
            o = _window_attention(tok, kt, kmt, vm, lw["sink"], lw["bias"], WINDOW_ROWS)
        else:
            o = _nbr_attention(tok, kt, kmt, vm, lw["bias"])
        x = _post_attention(x, o, lw["w_o"], ffn, i, g_final, POST_ROWS, final_norm=(i == DEPTH - 1))
    return x


def kernel(x_prompt, x_sample, mem_prompt, mem_sample, g_mix, g_mem, w_in_a, sink_a, w_in_b, rpb_b, w_mem_kv, w_o,
           g_ffn, w_gate, w_up, conv_w, conv_b, w_down, g_final):
    layers, ffn, mem_weights = _prepare_weights(g_mix, g_mem, w_in_a, sink_a, w_in_b, rpb_b, w_mem_kv, w_o, g_ffn,
                                                w_gate, w_up, conv_w, conv_b, w_down, g_final)
    y_prompt = _trunk(x_prompt, mem_prompt, layers, ffn, mem_weights, g_final)
    y_sample = _trunk(x_sample, mem_sample, layers, ffn, mem_weights, g_final)
    return (y_prompt, y_sample)
```

```python
import functools

import numpy as np
import jax
import jax.numpy as jnp
from jax import lax
from jax.experimental import pallas as pl
from jax.experimental.pallas import tpu as pltpu

D_MODEL = 1024
DEPTH = 2
HEAD_DIM = 64
N_Q_HEADS = 12
N_KV_HEADS_A = 4
N_MEM_HEADS = 4
MEM_LEN = 256
WINDOW = 128
GRID_W = 64
NA_ROWS = 8
NA_COLS = 16
D_FF = 2816
CONV_W = 3
EPS = 1e-6
NEG = -1e30
Q_WIDTH = N_Q_HEADS * HEAD_DIM
KV_WIDTH_A = N_KV_HEADS_A * HEAD_DIM
MEM_WIDTH = N_MEM_HEADS * HEAD_DIM
MIX_WIDTH = Q_WIDTH + MEM_WIDTH
LOG2E = 1.4426950408889634
Q_SCALE = HEAD_DIM ** -0.5 * LOG2E
REP_A = N_Q_HEADS // N_KV_HEADS_A

F32 = jnp.float32
BF16 = jnp.bfloat16

LANES = 128
VMEM_LIMIT_BYTES = 60 * 1024 * 1024
MXU_WIDTH = 256
FF_CHUNK_BOUNDS = (0, 6 * MXU_WIDTH, D_FF)
HALO = 16

PROJ_ROWS = 1024
WINDOW_ROWS = 1024
POST_ROWS = 1024

_NT = (((1,), (1,)), ((), ()))


def _params(*sem):
    return pltpu.CompilerParams(dimension_semantics=sem, vmem_limit_bytes=VMEM_LIMIT_BYTES)


def _resident(shape, index_map):
    return pl.BlockSpec(shape, index_map, pipeline_mode=pl.Buffered(1))


def _rms(x, g):
    return x * lax.rsqrt(jnp.mean(x * x, axis=-1, keepdims=True) + EPS) * g


PROJ_PARTS = 2


def _proj_kernel(x_ref, g_ref, w_ref, cs_ref, wt_ref, tok_ref, feat_ref):
    part = x_ref.shape[0] // PROJ_PARTS
    for r in range(PROJ_PARTS):
        rows = slice(r * part, (r + 1) * part)
        h = _rms(x_ref[rows, :], g_ref[...]).astype(BF16)
        tok = jnp.dot(h, w_ref[...], preferred_element_type=F32) * cs_ref[...]
        tok_ref[rows, :] = tok.astype(tok_ref.dtype)
        feat_ref[0, :, rows] = lax.dot_general(wt_ref[...], h, _NT, preferred_element_type=F32).astype(feat_ref.dtype)


def _norm_project(x, g, w_tok, col_scale, w_feat_t, batch, tm):
    n, d = x.shape
    t = n // batch
    per_seq = t // tm
    wdt = w_tok.shape[1]
    wft = w_feat_t.shape[0]
    return pl.pallas_call(
        _proj_kernel,
        grid=(n // tm,),
        in_specs=[
            pl.BlockSpec((tm, d), lambda i: (i, 0)),
            _resident((1, d), lambda i: (0, 0)),
            _resident((d, wdt), lambda i: (0, 0)),
            _resident((1, wdt), lambda i: (0, 0)),
            _resident((wft, d), lambda i: (0, 0)),
        ],
        out_specs=[pl.BlockSpec((tm, wdt), lambda i: (i, 0)),
                   pl.BlockSpec((1, wft, tm), lambda i: (i // per_seq, 0, i % per_seq))],
        out_shape=[jax.ShapeDtypeStruct((n, wdt), BF16), jax.ShapeDtypeStruct((batch, wft, t), BF16)],
        compiler_params=_params("parallel"),
        name="norm_project",
    )(x, g.reshape(1, d), w_tok, col_scale.reshape(1, wdt), w_feat_t)


def _mem_proj_kernel(m_ref, g_ref, wv_ref, wkt_ref, *out_refs):
    x = m_ref[...]
    y = x * lax.rsqrt(jnp.mean(x * x, axis=-1, keepdims=True) + EPS)
    n_seq = m_ref.shape[0] // MEM_LEN
    for l in range(DEPTH):
        vm_ref, kmt_ref = out_refs[2 * l], out_refs[2 * l + 1]
        h = (y * g_ref[l:l + 1, :]).astype(BF16)
        vm_ref[...] = jnp.dot(h, wv_ref[l], preferred_element_type=F32).astype(vm_ref.dtype)
        for s in range(n_seq):
            kmt_ref[s] = lax.dot_general(wkt_ref[l], h[s * MEM_LEN:(s + 1) * MEM_LEN], _NT,
                                         preferred_element_type=F32).astype(kmt_ref.dtype)


def _memory_project(mem, g_mem, w_vm, w_kmt):
    b, _, d = mem.shape
    n_seq = next(c for c in (4, 2, 1) if b % c == 0)
    tm = n_seq * MEM_LEN
    outs = pl.pallas_call(
        _mem_proj_kernel,
        grid=(b // n_seq,),
        in_specs=[
            pl.BlockSpec((tm, d), lambda i: (i, 0)),
            _resident(g_mem.shape, lambda i: (0, 0)),
            _resident(w_vm.shape, lambda i: (0, 0, 0)),
            _resident(w_kmt.shape, lambda i: (0, 0, 0)),
        ],
        out_specs=[pl.BlockSpec((tm, MEM_WIDTH), lambda i: (i, 0)),
                   pl.BlockSpec((n_seq, MEM_WIDTH, MEM_LEN), lambda i: (i, 0, 0))] * DEPTH,
        out_shape=[jax.ShapeDtypeStruct((b * MEM_LEN, MEM_WIDTH), BF16),
                   jax.ShapeDtypeStruct((b, MEM_WIDTH, MEM_LEN), BF16)] * DEPTH,
        compiler_params=_params("parallel"),
        name="memory_project",
    )(mem.reshape(b * MEM_LEN, d), g_mem, w_vm, w_kmt)
    return [(outs[2 * l].reshape(b, MEM_LEN, MEM_WIDTH), outs[2 * l + 1]) for l in range(DEPTH)]


def _low_lanes(rows):
    return lax.broadcasted_iota(jnp.int32, (rows, LANES), 1) < HEAD_DIM


def _split_heads(q_tile, low):
    zero = jnp.zeros_like(q_tile)
    return jnp.where(low, q_tile, zero), jnp.where(low, zero, q_tile)


def _with_ones(v):
    return jnp.concatenate([v, jnp.ones_like(v)], axis=1)


def _merge_heads(oa, low, lo_rows, hi_rows, extra_den=None):
    pick = lambda a: jnp.where(low, a[lo_rows], a[hi_rows])
    den = pick(oa[:, LANES:])
    if extra_den is not None:
        den = den + pick(extra_den)
    return pick(oa[:, :LANES]) / den


def _run_pipelined(stages, ahead):
    pending = [stages[k][0]() for k in range(min(ahead, len(stages)))]
    for k, (_, finish) in enumerate(stages):
        if k + ahead < len(stages):
            pending.append(stages[k + ahead][0]())
        finish(pending.pop(0))


def _memory_stages(qm_ref, kmt_ref, vm_ref, o_ref, tq):
    low = _low_lanes(tq)
    stages = []
    for r in range(qm_ref.shape[1] // tq):
        rows = slice(r * tq, (r + 1) * tq)
        for p in range(MEM_WIDTH // LANES):
            cols = slice(p * LANES, (p + 1) * LANES)

            def scores(rows=rows, cols=cols):
                qs = jnp.concatenate(_split_heads(qm_ref[0, rows, cols], low), axis=0)
                return jnp.dot(qs, kmt_ref[0, cols, :], preferred_element_type=F32)

            def finish(s, rows=rows, cols=cols, p=p):
                pr = jnp.exp2(s - jnp.max(s, axis=-1, keepdims=True)).astype(BF16)
                oa = jnp.dot(pr, _with_ones(vm_ref[0, :, cols]), preferred_element_type=F32)
                o_ref[0, rows, Q_WIDTH + p * LANES:Q_WIDTH + (p + 1) * LANES] = (
                    _merge_heads(oa, low, slice(0, tq), slice(tq, 2 * tq)).astype(o_ref.dtype))

            stages.append((scores, finish))
    return stages


WINDOW_TILES_PER_STAGE = 1
WINDOW_LOOKAHEAD = 3


def _window_head(pair, row_block):
    return REP_A * (2 * pair + row_block % 2) + row_block // 2


def _window_head_order():
    return [_window_head(p, rb) for p in range(N_KV_HEADS_A // 2) for rb in range(2 * REP_A)]


def _window_bias_table():
    slopes = (2.0 ** (-8.0 * np.arange(1, N_Q_HEADS + 1, dtype=np.float32) / N_Q_HEADS)).astype(np.float32)
    blk = WINDOW
    qq = np.arange(blk)[:, None]
    kk = np.arange(3 * blk)[None, :]
    arel = np.abs(qq - kk + WINDOW)
    in_window = arel <= WINDOW
    in_seq = [kk >= blk, np.ones_like(kk, bool), kk < 2 * blk]
    out = np.empty((3, N_KV_HEADS_A // 2, 2 * REP_A, blk, 3 * blk), np.float32)
    for var in range(3):
        for p in range(N_KV_HEADS_A // 2):
            for rb in range(2 * REP_A):
                alibi = -slopes[_window_head(p, rb)] * arel.astype(np.float32) * np.float32(LOG2E)
                out[var, p, rb] = np.where(in_window & in_seq[var], alibi, np.float32(NEG))
    return out.reshape(3 * (N_KV_HEADS_A // 2), 2 * REP_A * blk, 3 * blk)


def _window_kernel(sink_ref, q_ref, ktp_ref, ktc_ref, ktn_ref, vp_ref, vc_ref, vn_ref, qm_ref, kmt_ref, vm_ref,
                   bias_ref, o_ref, ktbuf, vbuf, *, tq, n_blk):
    i = pl.program_id(1)
    blk = WINDOW
    n_pair = N_KV_HEADS_A // 2
    ktbuf[:, 0:blk] = ktp_ref[0]
    ktbuf[:, blk:blk + tq] = ktc_ref[0]
    ktbuf[:, blk + tq:] = ktn_ref[0]
    vbuf[0:blk] = vp_ref[0]
    vbuf[blk:blk + tq] = vc_ref[0]
    vbuf[blk + tq:] = vn_ref[0]

    low = _low_lanes(blk)
    sinks = [jnp.concatenate([jnp.full((blk, LANES), sink_ref[_window_head(p, rb)], F32)
                              for rb in range(2 * REP_A)], axis=0) for p in range(n_pair)]
    nt = WINDOW_TILES_PER_STAGE

    def scores(j, p, t0):
        g_blk = i * (tq // blk) + j
        variant = jnp.where(g_blk == 0, 0, jnp.where(g_blk == n_blk - 1, 2, 1))
        tiles = []
        for t in range(t0, t0 + nt):
            c0 = (REP_A * p + t) * LANES
            tiles.extend(_split_heads(q_ref[0, j * blk:(j + 1) * blk, c0:c0 + LANES], low))
        qs = jnp.concatenate(tiles, axis=0)
        sc = jnp.dot(qs, ktbuf[p * LANES:(p + 1) * LANES, j * blk:(j + 3) * blk], preferred_element_type=F32)
        return sc + bias_ref[variant * n_pair + p, 2 * t0 * blk:2 * (t0 + nt) * blk, :]

    def finish(j, p, t0, sc):
        sink = sinks[p][2 * t0 * blk:2 * (t0 + nt) * blk]
        m = jnp.maximum(jnp.max(sc, axis=-1, keepdims=True), sink)
        pr = jnp.concatenate([jnp.exp2(sc[:, c * LANES:(c + 1) * LANES] - m) for c in range(3 * blk // LANES)],
                             axis=1).astype(BF16)
        oa = jnp.dot(pr, _with_ones(vbuf[j * blk:(j + 3) * blk, p * LANES:(p + 1) * LANES]),
                     preferred_element_type=F32)
        sink_mass = jnp.exp2(sink - m)
        for k in range(nt):
            c0 = (REP_A * p + t0 + k) * LANES
            o_ref[0, j * blk:(j + 1) * blk, c0:c0 + LANES] = _merge_heads(
                oa, low, slice(2 * k * blk, (2 * k + 1) * blk), slice((2 * k + 1) * blk, (2 * k + 2) * blk),
                extra_den=sink_mass).astype(o_ref.dtype)

    stages = [(functools.partial(scores, j, p, t0), functools.partial(finish, j, p, t0))
              for j in range(tq // blk) for p in range(n_pair) for t0 in range(0, REP_A, nt)]
    _run_pipelined(stages + _memory_stages(qm_ref, kmt_ref, vm_ref, o_ref, 2 * blk), WINDOW_LOOKAHEAD)


def _window_attention(tok, kt, kmt, vm, sink, bias, tq):
    b, t, _ = tok.shape
    blk = WINDOW
    n_blk = t // blk
    per = tq // blk
    assert n_blk >= 2 and t % tq == 0
    vcol = Q_WIDTH // KV_WIDTH_A
    prev_i = lambda i: jnp.maximum(i * per - 1, 0)
    next_i = lambda i: jnp.minimum((i + 1) * per, n_blk - 1)
    kernel = functools.partial(_window_kernel, tq=tq, n_blk=n_blk)
    return pl.pallas_call(
        kernel,
        grid=(b, t // tq),
        in_specs=[
            pl.BlockSpec(memory_space=pltpu.SMEM),
            pl.BlockSpec((1, tq, Q_WIDTH), lambda bi, i: (bi, i, 0)),
            pl.BlockSpec((1, KV_WIDTH_A, blk), lambda bi, i: (bi, 0, prev_i(i))),
            pl.BlockSpec((1, KV_WIDTH_A, tq), lambda bi, i: (bi, 0, i)),
            pl.BlockSpec((1, KV_WIDTH_A, blk), lambda bi, i: (bi, 0, next_i(i))),
            pl.BlockSpec((1, blk, KV_WIDTH_A), lambda bi, i: (bi, prev_i(i), vcol)),
            pl.BlockSpec((1, tq, KV_WIDTH_A), lambda bi, i: (bi, i, vcol)),
            pl.BlockSpec((1, blk, KV_WIDTH_A), lambda bi, i: (bi, next_i(i), vcol)),
            pl.BlockSpec((1, tq, MEM_WIDTH), lambda bi, i: (bi, i, vcol + 1)),
            pl.BlockSpec((1, MEM_WIDTH, MEM_LEN), lambda bi, i: (bi, 0, 0)),
            pl.BlockSpec((1, MEM_LEN, MEM_WIDTH), lambda bi, i: (bi, 0, 0)),
            _resident(bias.shape, lambda bi, i: (0, 0, 0)),
        ],
        out_specs=pl.BlockSpec((1, tq, MIX_WIDTH), lambda bi, i: (bi, i, 0)),
        out_shape=jax.ShapeDtypeStruct((b, t, MIX_WIDTH), BF16),
        scratch_shapes=[pltpu.VMEM((KV_WIDTH_A, tq + 2 * blk), BF16),
                        pltpu.VMEM((tq + 2 * blk, KV_WIDTH_A), BF16)],
        compiler_params=_params("parallel", "parallel"),
        name="window_attention",
    )(sink, tok, kt, kt, kt, tok, tok, tok, tok, kmt, vm, bias)


NB_ROWS_PER_TILE = 4
NB_TQ = NB_ROWS_PER_TILE * GRID_W
NB_TILES_PER_STEP = 4
NB_LOOKAHEAD = 2


def _nbr_bias_table(rpb):
    h, n_dr, n_dc = rpb.shape
    w = GRID_W
    dc = np.arange(w)[None, :] - np.arange(w)[:, None] + NA_COLS - 1
    onehot = (dc[None] == np.arange(n_dc)[:, None, None]).astype(np.float32)
    blocks = jnp.einsum("hrd,dqk->hqrk", rpb.astype(F32), jnp.asarray(onehot),
                        precision=lax.Precision.HIGHEST)
    n_kr = 3 * NB_ROWS_PER_TILE
    starts = [(NA_ROWS - 1) - NB_ROWS_PER_TILE - a for a in range(NB_ROWS_PER_TILE)]
    assert min(starts) >= 0 and max(starts) + n_kr <= n_dr
    strips = [blocks[:, :, s:s + n_kr, :].reshape(h, w, n_kr * w) for s in starts]
    return jnp.stack(strips, axis=1).reshape(h, NB_TQ, 3 * NB_TQ)


def _nbr_mask_table(n_tiles, n_rows):
    qi = np.arange(NB_TQ)[:, None]
    ki = np.arange(3 * NB_TQ)[None, :]
    c0 = np.clip(qi % GRID_W - NA_COLS // 2, 0, GRID_W - NA_COLS)
    k_col = ki % GRID_W
    col_ok = (k_col >= c0) & (k_col < c0 + NA_COLS)
    out = []
    for i in (0, min(1, n_tiles - 1), n_tiles - 1):
        q_row = i * NB_ROWS_PER_TILE + qi // GRID_W
        k_row = (i - 1) * NB_ROWS_PER_TILE + ki // GRID_W
        r0 = np.clip(q_row - NA_ROWS // 2, 0, n_rows - NA_ROWS)
        row_ok = (k_row >= r0) & (k_row < r0 + NA_ROWS)
        out.append(np.where(row_ok & col_ok, np.float32(0.0), np.float32(NEG)))
    return np.stack(out).astype(np.float32)


_NBR_CANONICAL_TILES = 3


def _nbr_logit_table(rpb):
    mask = _nbr_mask_table(_NBR_CANONICAL_TILES, _NBR_CANONICAL_TILES * NB_ROWS_PER_TILE)
    table = (_nbr_bias_table(rpb)[None] + jnp.asarray(mask)[:, None]) * LOG2E
    return table.reshape(3 * N_Q_HEADS, NB_TQ, 3 * NB_TQ)


def _nbr_kernel(q_ref, ktp_ref, ktc_ref, ktn_ref, vp_ref, vc_ref, vn_ref, qm_ref, kmt_ref, vm_ref, table_ref,
                o_ref, *, n_tiles):
    i = pl.program_id(1)
    tq = NB_TQ
    per = NB_TILES_PER_STEP
    low = _low_lanes(tq)
    kt_tiles = ([lambda cols: ktp_ref[0, cols, :]]
                + [(lambda cols, s=s: ktc_ref[0, cols, s * tq:(s + 1) * tq]) for s in range(per)]
                + [lambda cols: ktn_ref[0, cols, :]])
    v_tiles = ([lambda cols: vp_ref[0, :, cols]]
               + [(lambda cols, s=s: vc_ref[0, s * tq:(s + 1) * tq, cols]) for s in range(per)]
               + [lambda cols: vn_ref[0, :, cols]])
    def scores(s, p, hi):
        tile = i * per + s
        variant = jnp.where(tile == 0, 0, jnp.where(tile == n_tiles - 1, 2, 1))
        cols = slice(p * LANES, (p + 1) * LANES)
        qs = _split_heads(q_ref[0, s * tq:(s + 1) * tq, cols], low)[hi]
        kt = jnp.concatenate([f(cols) for f in kt_tiles[s:s + 3]], axis=1)
        sc = jnp.dot(qs, kt, preferred_element_type=F32)
        return sc + table_ref[variant * N_Q_HEADS + 2 * p + hi]

    pending = {}

    def finish(s, p, hi, sc):
        cols = slice(p * LANES, (p + 1) * LANES)
        pr = jnp.exp2(sc - jnp.max(sc, axis=-1, keepdims=True)).astype(BF16)
        v = jnp.concatenate([f(cols) for f in v_tiles[s:s + 3]], axis=0)
        oa = jnp.dot(pr, _with_ones(v), preferred_element_type=F32)
        if not hi:
            pending[(s, p)] = oa
            return
        both = jnp.concatenate([pending.pop((s, p)), oa], axis=0)
        o_ref[0, s * tq:(s + 1) * tq, cols] = (
            _merge_heads(both, low, slice(0, tq), slice(tq, 2 * tq)).astype(o_ref.dtype))

    stages = [(functools.partial(scores, s, p, hi), functools.partial(finish, s, p, hi))
              for s in range(per) for p in range(Q_WIDTH // LANES) for hi in range(2)]
    _run_pipelined(stages + _memory_stages(qm_ref, kmt_ref, vm_ref, o_ref, tq), NB_LOOKAHEAD)


def _nbr_attention(tok, kt, kmt, vm, table):
    b, t, _ = tok.shape
    tq = NB_TQ
    n = t // tq
    n_rows = t // GRID_W
    per = NB_TILES_PER_STEP
    ts = per * tq
    assert n_rows >= NA_ROWS and t % ts == 0 and n >= 2
    canonical = _nbr_mask_table(_NBR_CANONICAL_TILES, _NBR_CANONICAL_TILES * NB_ROWS_PER_TILE)
    assert n == 2 or np.array_equal(_nbr_mask_table(n, n_rows), canonical)
    assert n > 2 or np.array_equal(_nbr_mask_table(n, n_rows)[[0, 2]], canonical[[0, 2]])
    prev_i = lambda i: jnp.maximum(i * per - 1, 0)
    next_i = lambda i: jnp.minimum((i + 1) * per, n - 1)
    return pl.pallas_call(
        functools.partial(_nbr_kernel, n_tiles=n),
        grid=(b, t // ts),
        in_specs=[
            pl.BlockSpec((1, ts, Q_WIDTH), lambda bi, i: (bi, i, 0)),
            pl.BlockSpec((1, Q_WIDTH, tq), lambda bi, i: (bi, 0, prev_i(i))),
            pl.BlockSpec((1, Q_WIDTH, ts), lambda bi, i: (bi, 0, i)),
            pl.BlockSpec((1, Q_WIDTH, tq), lambda bi, i: (bi, 0, next_i(i))),
            pl.BlockSpec((1, tq, Q_WIDTH), lambda bi, i: (bi, prev_i(i), 1)),
            pl.BlockSpec((1, ts, Q_WIDTH), lambda bi, i: (bi, i, 1)),
            pl.BlockSpec((1, tq, Q_WIDTH), lambda bi, i: (bi, next_i(i), 1)),
            pl.BlockSpec((1, ts, MEM_WIDTH), lambda bi, i: (bi, i, 2 * Q_WIDTH // MEM_WIDTH)),
            pl.BlockSpec((1, MEM_WIDTH, MEM_LEN), lambda bi, i: (bi, 0, 0)),
            pl.BlockSpec((1, MEM_LEN, MEM_WIDTH), lambda bi, i: (bi, 0, 0)),
            _resident(table.shape, lambda bi, i: (0, 0, 0)),
        ],
        out_specs=pl.BlockSpec((1, ts, MIX_WIDTH), lambda bi, i: (bi, i, 0)),
        out_shape=jax.ShapeDtypeStruct((b, t, MIX_WIDTH), BF16),
        compiler_params=_params("parallel", "parallel"),
        name="nbr_attention",
    )(tok, kt, kt, kt, tok, tok, tok, tok, kmt, vm, table)


def _post_kernel(xp_ref, xc_ref, xn_ref, op_ref, oc_ref, on_ref, wo_ref, gn_ref, wg_ref, wu_ref, cw_ref, cb_ref,
                 wd_ref, gf_ref, y_ref, oext_ref, hext_ref, act_ref, *, final_norm):
    i = pl.program_id(1)
    n = pl.num_programs(1)
    tm = xc_ref.shape[1]
    ext = tm + HALO
    halo_row = lax.broadcasted_iota(jnp.int32, (HALO, 1), 0)
    take_next = halo_row < HALO // 2
    oext_ref[0:tm] = oc_ref[0]
    oext_ref[tm:] = jnp.where(take_next, on_ref[0], op_ref[0])
    mix = jnp.dot(oext_ref[...], wo_ref[...], preferred_element_type=F32)
    gn = gn_ref[...]
    x1 = xc_ref[0] + mix[0:tm]
    y_ref[0] = x1
    hext_ref[0:tm] = _rms(x1, gn).astype(BF16)
    h_halo = _rms(jnp.where(take_next, xn_ref[0], xp_ref[0]) + mix[tm:], gn).astype(BF16)
    halo_ok = ((halo_row == 0) & (i < n - 1)) | ((halo_row == HALO - 1) & (i > 0))
    hext_ref[tm:] = jnp.where(halo_ok, h_halo, jnp.zeros_like(h_halo))

    for lo, hi in zip(FF_CHUNK_BOUNDS[:-1], FF_CHUNK_BOUNDS[1:]):
        cols = slice(lo, hi)
        g_ext = jnp.dot(hext_ref[...], wg_ref[:, cols], preferred_element_type=F32)
        g = g_ext[0:tm]
        g_prev = pltpu.roll(g_ext, 1, 0)[0:tm]
        g_next = pltpu.roll(g_ext, ext - 1, 0)[0:tm]
        gc = g_prev * cw_ref[0:1, cols] + g * cw_ref[1:2, cols] + g_next * cw_ref[2:3, cols] + cb_ref[:, cols]
        u = jnp.dot(hext_ref[0:tm], wu_ref[:, cols], preferred_element_type=F32)
        act_ref[:, cols] = (gc * jax.nn.sigmoid(gc) * u).astype(BF16)

    y = y_ref[0] + jnp.dot(act_ref[...], wd_ref[...], preferred_element_type=F32)
    if final_norm:
        y = _rms(y, gf_ref[...])
    y_ref[0] = y


def _post_attention(x, o, w_o, ffn, layer, g_final, tm, final_norm):
    b, t, d = x.shape
    per = tm // HALO
    n_halo = t // HALO
    kernel = functools.partial(_post_kernel, final_norm=final_norm)
    const2 = lambda bi, i: (0, 0)
    of_layer = lambda a: pl.BlockSpec((None,) + a.shape[1:], lambda bi, i: (layer, 0, 0),
                                      pipeline_mode=pl.Buffered(1))
    prev_map = lambda bi, i: (bi, jnp.maximum(i * per - 1, 0), 0)
    cur_map = lambda bi, i: (bi, i, 0)
    next_map = lambda bi, i: (bi, jnp.minimum((i + 1) * per, n_halo - 1), 0)
    return pl.pallas_call(
        kernel,
        grid=(b, t // tm),
        in_specs=[
            pl.BlockSpec((1, HALO, d), prev_map),
            pl.BlockSpec((1, tm, d), cur_map),
            pl.BlockSpec((1, HALO, d), next_map),
            pl.BlockSpec((1, HALO, MIX_WIDTH), prev_map),
            pl.BlockSpec((1, tm, MIX_WIDTH), cur_map),
            pl.BlockSpec((1, HALO, MIX_WIDTH), next_map),
            _resident(w_o.shape, const2),
            of_layer(ffn["g_ffn"]),
            of_layer(ffn["w_gate"]),
            of_layer(ffn["w_up"]),
            of_layer(ffn["conv_w"]),
            of_layer(ffn["conv_b"]),
            of_layer(ffn["w_down"]),
            _resident((1, d), const2),
        ],
        out_specs=pl.BlockSpec((1, tm, d), cur_map),
        out_shape=jax.ShapeDtypeStruct((b, t, d), F32),
        scratch_shapes=[pltpu.VMEM((tm + HALO, MIX_WIDTH), BF16), pltpu.VMEM((tm + HALO, d), BF16),
                        pltpu.VMEM((tm, D_FF), BF16)],
        compiler_params=_params("parallel", "parallel"),
        name="post_attention",
    )(x, x, x, o, o, o, w_o, ffn["g_ffn"], ffn["w_gate"], ffn["w_up"], ffn["conv_w"], ffn["conv_b"], ffn["w_down"],
      g_final.reshape(1, d))


def _prepare_weights(g_mix, g_mem, w_in_a, sink_a, w_in_b, rpb_b, w_mem_kv, w_o, g_ffn, w_gate, w_up, conv_w,
                     conv_b, w_down, g_final):
    head_cols = np.arange(Q_WIDTH).reshape(N_Q_HEADS, HEAD_DIM)
    perm_a = head_cols[_window_head_order()].reshape(-1)
    layers = []
    for i in range(DEPTH):
        j = i // 2
        wo = w_o[i]
        if i % 2 == 0:
            w = w_in_a[j]
            wq, wk, wv, wqm = (w[:, :Q_WIDTH], w[:, Q_WIDTH:Q_WIDTH + KV_WIDTH_A],
                               w[:, Q_WIDTH + KV_WIDTH_A:Q_WIDTH + 2 * KV_WIDTH_A], w[:, Q_WIDTH + 2 * KV_WIDTH_A:])
            wq = wq[:, perm_a]
            wo = jnp.concatenate([wo[:Q_WIDTH][perm_a], wo[Q_WIDTH:]], axis=0)
            extra = dict(sink=sink_a[j].astype(F32) * LOG2E, bias=jnp.asarray(_window_bias_table()))
        else:
            w = w_in_b[j]
            wq, wk, wv, wqm = (w[:, :Q_WIDTH], w[:, Q_WIDTH:2 * Q_WIDTH], w[:, 2 * Q_WIDTH:3 * Q_WIDTH],
                               w[:, 3 * Q_WIDTH:])
            extra = dict(bias=_nbr_logit_table(rpb_b[j]))
        tok_scale = np.concatenate([np.full(wq.shape[1], Q_SCALE), np.ones(wv.shape[1]),
                                    np.full(wqm.shape[1], Q_SCALE)]).astype(np.float32)
        layers.append(dict(
            g_mix=g_mix[i],
            w_tok=jnp.concatenate([wq, wv, wqm], axis=1).astype(BF16), tok_scale=jnp.asarray(tok_scale),
            w_kt=wk.T.astype(BF16),
            w_o=wo.astype(BF16), **extra))
    ffn = dict(g_ffn=g_ffn.reshape(DEPTH, 1, D_MODEL), w_gate=w_gate.astype(BF16), w_up=w_up.astype(BF16),
               conv_w=conv_w, conv_b=conv_b.reshape(DEPTH, 1, D_FF), w_down=w_down.astype(BF16))
    mem_weights = dict(g_mem=g_mem, w_vm=w_mem_kv[:, :, MEM_WIDTH:].astype(BF16),
                       w_kmt=jnp.swapaxes(w_mem_kv[:, :, :MEM_WIDTH], 1, 2).astype(BF16))
    return layers, ffn, mem_weights


def _trunk(x, mem, layers, ffn, mem_weights, g_final):
    b, t, d = x.shape
    n = b * t
    assert t % PROJ_ROWS == 0 and t % WINDOW_ROWS == 0 and t % POST_ROWS == 0
    mem_kv = _memory_project(mem, mem_weights["g_mem"], mem_weights["w_vm"], mem_weights["w_kmt"])
    for i, lw in enumerate(layers):
        tok, kt = _norm_project(x.reshape(n, d), lw["g_mix"], lw["w_tok"], lw["tok_scale"], lw["w_kt"], b, PROJ_ROWS)
        tok = tok.reshape(b, t, -1)
        vm, kmt = mem_kv[i]
        if i % 2 == 0:
            o = _window_attention(tok, kt, kmt, vm, lw["sink"], lw["bias"], WINDOW_ROWS)
        else:
            o = _nbr_attention(tok, kt, kmt, vm, lw["bias"])
        x = _post_attention(x, o, lw["w_o"], ffn, i, g_final, POST_ROWS, final_norm=(i == DEPTH - 1))
    return x


def kernel(x_prompt, x_sample, mem_prompt, mem_sample, g_mix, g_mem, w_in_a, sink_a, w_in_b, rpb_b, w_mem_kv, w_o,
           g_ffn, w_gate, w_up, conv_w, conv_b, w_down, g_final):
    layers, ffn, mem_weights = _prepare_weights(g_mix, g_mem, w_in_a, sink_a, w_in_b, rpb_b, w_mem_kv, w_o, g_ffn,
                                                w_gate, w_up, conv_w, conv_b, w_down, g_final)
    y_prompt = _trunk(x_prompt, mem_prompt, layers, ffn, mem_weights, g_final)
    y_sample = _trunk(x_sample, mem_sample, layers, ffn, mem_weights, g_final)
    return (y_prompt, y_sample)
```

```python
import functools

import numpy as np
import jax
import jax.numpy as jnp
from jax import lax
from jax.experimental import pallas as pl
from jax.experimental.pallas import tpu as pltpu

D_MODEL = 1024
DEPTH = 2
HEAD_DIM = 64
N_Q_HEADS = 12
N_KV_HEADS_A = 4
N_MEM_HEADS = 4
MEM_LEN = 256
WINDOW = 128
GRID_W = 64
NA_ROWS = 8
NA_COLS = 16
D_FF = 2816
CONV_W = 3
EPS = 1e-6
NEG = -1e30
Q_WIDTH = N_Q_HEADS * HEAD_DIM
KV_WIDTH_A = N_KV_HEADS_A * HEAD_DIM
MEM_WIDTH = N_MEM_HEADS * HEAD_DIM
MIX_WIDTH = Q_WIDTH + MEM_WIDTH
LOG2E = 1.4426950408889634
Q_SCALE = HEAD_DIM ** -0.5 * LOG2E
REP_A = N_Q_HEADS // N_KV_HEADS_A

F32 = jnp.float32
BF16 = jnp.bfloat16

LANES = 128
VMEM_LIMIT_BYTES = 60 * 1024 * 1024
MXU_WIDTH = 256
FF_CHUNK_BOUNDS = (0, 6 * MXU_WIDTH, D_FF)
HALO = 16

PROJ_ROWS = 1024
WINDOW_ROWS = 1024
POST_ROWS = 1024

_NT = (((1,), (1,)), ((), ()))


def _params(*sem):
    return pltpu.CompilerParams(dimension_semantics=sem, vmem_limit_bytes=VMEM_LIMIT_BYTES)


def _resident(shape, index_map):
    return pl.BlockSpec(shape, index_map, pipeline_mode=pl.Buffered(1))


def _rms(x, g):
    return x * lax.rsqrt(jnp.mean(x * x, axis=-1, keepdims=True) + EPS) * g


PROJ_PARTS = 2


def _proj_kernel(x_ref, g_ref, w_ref, cs_ref, wt_ref, tok_ref, feat_ref):
    part = x_ref.shape[0] // PROJ_PARTS
    for r in range(PROJ_PARTS):
        rows = slice(r * part, (r + 1) * part)
        h = _rms(x_ref[rows, :], g_ref[...]).astype(BF16)
        tok = jnp.dot(h, w_ref[...], preferred_element_type=F32) * cs_ref[...]
        tok_ref[rows, :] = tok.astype(tok_ref.dtype)
        feat_ref[0, :, rows] = lax.dot_general(wt_ref[...], h, _NT, preferred_element_type=F32).astype(feat_ref.dtype)


def _norm_project(x, g, w_tok, col_scale, w_feat_t, batch, tm):
    n, d = x.shape
    t = n // batch
    per_seq = t // tm
    wdt = w_tok.shape[1]
    wft = w_feat_t.shape[0]
    return pl.pallas_call(
        _proj_kernel,
        grid=(n // tm,),
        in_specs=[
            pl.BlockSpec((tm, d), lambda i: (i, 0)),
            _resident((1, d), lambda i: (0, 0)),
            _resident((d, wdt), lambda i: (0, 0)),
            _resident((1, wdt), lambda i: (0, 0)),
            _resident((wft, d), lambda i: (0, 0)),
        ],
        out_specs=[pl.BlockSpec((tm, wdt), lambda i: (i, 0)),
                   pl.BlockSpec((1, wft, tm), lambda i: (i // per_seq, 0, i % per_seq))],
        out_shape=[jax.ShapeDtypeStruct((n, wdt), BF16), jax.ShapeDtypeStruct((batch, wft, t), BF16)],
        compiler_params=_params("parallel"),
        name="norm_project",
    )(x, g.reshape(1, d), w_tok, col_scale.reshape(1, wdt), w_feat_t)


def _mem_proj_kernel(m_ref, g_ref, wv_ref, wkt_ref, *out_refs):
    x = m_ref[...]
    y = x * lax.rsqrt(jnp.mean(x * x, axis=-1, keepdims=True) + EPS)
    n_seq = m_ref.shape[0] // MEM_LEN
    for l in range(DEPTH):
        vm_ref, kmt_ref = out_refs[2 * l], out_refs[2 * l + 1]
        h = (y * g_ref[l:l + 1, :]).astype(BF16)
        vm_ref[...] = jnp.dot(h, wv_ref[l], preferred_element_type=F32).astype(vm_ref.dtype)
        for s in range(n_seq):
            kmt_ref[s] = lax.dot_general(wkt_ref[l], h[s * MEM_LEN:(s + 1) * MEM_LEN], _NT,
                                         preferred_element_type=F32).astype(kmt_ref.dtype)


def _memory_project(mem, g_mem, w_vm, w_kmt):
    b, _, d = mem.shape
    n_seq = next(c for c in (4, 2, 1) if b % c == 0)
    tm = n_seq * MEM_LEN
    outs = pl.pallas_call(
        _mem_proj_kernel,
        grid=(b // n_seq,),
        in_specs=[
            pl.BlockSpec((tm, d), lambda i: (i, 0)),
            _resident(g_mem.shape, lambda i: (0, 0)),
            _resident(w_vm.shape, lambda i: (0, 0, 0)),
            _resident(w_kmt.shape, lambda i: (0, 0, 0)),
        ],
        out_specs=[pl.BlockSpec((tm, MEM_WIDTH), lambda i: (i, 0)),
                   pl.BlockSpec((n_seq, MEM_WIDTH, MEM_LEN), lambda i: (i, 0, 0))] * DEPTH,
        out_shape=[jax.ShapeDtypeStruct((b * MEM_LEN, MEM_WIDTH), BF16),
                   jax.ShapeDtypeStruct((b, MEM_WIDTH, MEM_LEN), BF16)] * DEPTH,
        compiler_params=_params("parallel"),
        name="memory_project",
    )(mem.reshape(b * MEM_LEN, d), g_mem, w_vm, w_kmt)
    return [(outs[2 * l].reshape(b, MEM_LEN, MEM_WIDTH), outs[2 * l + 1]) for l in range(DEPTH)]


def _low_lanes(rows):
    return lax.broadcasted_iota(jnp.int32, (rows, LANES), 1) < HEAD_DIM


def _split_heads(q_tile, low):
    zero = jnp.zeros_like(q_tile)
    return jnp.where(low, q_tile, zero), jnp.where(low, zero, q_tile)


def _with_ones(v):
    return jnp.concatenate([v, jnp.ones_like(v)], axis=1)


def _merge_heads(oa, low, lo_rows, hi_rows, extra_den=None):
    pick = lambda a: jnp.where(low, a[lo_rows], a[hi_rows])
    den = pick(oa[:, LANES:])
    if extra_den is not None:
        den = den + pick(extra_den)
    return pick(oa[:, :LANES]) / den


def _run_pipelined(stages, ahead):
    pending = [stages[k][0]() for k in range(min(ahead, len(stages)))]
    for k, (_, finish) in enumerate(stages):
        if k + ahead < len(stages):
            pending.append(stages[k + ahead][0]())
        finish(pending.pop(0))


def _memory_stages(qm_ref, kmt_ref, vm_ref, o_ref, tq):
    low = _low_lanes(tq)
    stages = []
    for r in range(qm_ref.shape[1] // tq):
        rows = slice(r * tq, (r + 1) * tq)
        for p in range(MEM_WIDTH // LANES):
            cols = slice(p * LANES, (p + 1) * LANES)

            def scores(rows=rows, cols=cols):
                qs = jnp.concatenate(_split_heads(qm_ref[0, rows, cols], low), axis=0)
                return jnp.dot(qs, kmt_ref[0, cols, :], preferred_element_type=F32)

            def finish(s, rows=rows, cols=cols, p=p):
                pr = jnp.exp2((s - jnp.max(s, axis=-1, keepdims=True)).astype(BF16))
                oa = jnp.dot(pr, _with_ones(vm_ref[0, :, cols]), preferred_element_type=F32)
                o_ref[0, rows, Q_WIDTH + p * LANES:Q_WIDTH + (p + 1) * LANES] = (
                    _merge_heads(oa, low, slice(0, tq), slice(tq, 2 * tq)).astype(o_ref.dtype))

            stages.append((scores, finish))
    return stages


WINDOW_TILES_PER_STAGE = 1
WINDOW_LOOKAHEAD = 3


def _window_head(pair, row_block):
    return REP_A * (2 * pair + row_block % 2) + row_block // 2


def _window_head_order():
    return [_window_head(p, rb) for p in range(N_KV_HEADS_A // 2) for rb in range(2 * REP_A)]


def _window_bias_table():
    slopes = (2.0 ** (-8.0 * np.arange(1, N_Q_HEADS + 1, dtype=np.float32) / N_Q_HEADS)).astype(np.float32)
    blk = WINDOW
    qq = np.arange(blk)[:, None]
    kk = np.arange(3 * blk)[None, :]
    arel = np.abs(qq - kk + WINDOW)
    in_window = arel <= WINDOW
    in_seq = [kk >= blk, np.ones_like(kk, bool), kk < 2 * blk]
    out = np.empty((3, N_KV_HEADS_A // 2, 2 * REP_A, blk, 3 * blk), np.float32)
    for var in range(3):
        for p in range(N_KV_HEADS_A // 2):
            for rb in range(2 * REP_A):
                alibi = -slopes[_window_head(p, rb)] * arel.astype(np.float32) * np.float32(LOG2E)
                out[var, p, rb] = np.where(in_window & in_seq[var], alibi, np.float32(NEG))
    return out.reshape(3 * (N_KV_HEADS_A // 2), 2 * REP_A * blk, 3 * blk)


def _window_kernel(sink_ref, q_ref, ktp_ref, ktc_ref, ktn_ref, vp_ref, vc_ref, vn_ref, qm_ref, kmt_ref, vm_ref,
                   bias_ref, o_ref, ktbuf, vbuf, *, tq, n_blk):
    i = pl.program_id(1)
    blk = WINDOW
    n_pair = N_KV_HEADS_A // 2
    ktbuf[:, 0:blk] = ktp_ref[0]
    ktbuf[:, blk:blk + tq] = ktc_ref[0]
    ktbuf[:, blk + tq:] = ktn_ref[0]
    vbuf[0:blk] = vp_ref[0]
    vbuf[blk:blk + tq] = vc_ref[0]
    vbuf[blk + tq:] = vn_ref[0]

    low = _low_lanes(blk)
    sinks = [jnp.concatenate([jnp.full((blk, LANES), sink_ref[_window_head(p, rb)], F32)
                              for rb in range(2 * REP_A)], axis=0) for p in range(n_pair)]
    nt = WINDOW_TILES_PER_STAGE

    def scores(j, p, t0):
        g_blk = i * (tq // blk) + j
        variant = jnp.where(g_blk == 0, 0, jnp.where(g_blk == n_blk - 1, 2, 1))
        tiles = []
        for t in range(t0, t0 + nt):
            c0 = (REP_A * p + t) * LANES
            tiles.extend(_split_heads(q_ref[0, j * blk:(j + 1) * blk, c0:c0 + LANES], low))
        qs = jnp.concatenate(tiles, axis=0)
        sc = jnp.dot(qs, ktbuf[p * LANES:(p + 1) * LANES, j * blk:(j + 3) * blk], preferred_element_type=F32)
        return sc + bias_ref[variant * n_pair + p, 2 * t0 * blk:2 * (t0 + nt) * blk, :]

    def finish(j, p, t0, sc):
        sink = sinks[p][2 * t0 * blk:2 * (t0 + nt) * blk]
        m = jnp.maximum(jnp.max(sc, axis=-1, keepdims=True), sink)
        pr = jnp.concatenate([jnp.exp2((sc[:, c * LANES:(c + 1) * LANES] - m).astype(BF16))
                              for c in range(3 * blk // LANES)], axis=1)
        oa = jnp.dot(pr, _with_ones(vbuf[j * blk:(j + 3) * blk, p * LANES:(p + 1) * LANES]),
                     preferred_element_type=F32)
        sink_mass = jnp.exp2(sink - m)
        for k in range(nt):
            c0 = (REP_A * p + t0 + k) * LANES
            o_ref[0, j * blk:(j + 1) * blk, c0:c0 + LANES] = _merge_heads(
                oa, low, slice(2 * k * blk, (2 * k + 1) * blk), slice((2 * k + 1) * blk, (2 * k + 2) * blk),
                extra_den=sink_mass).astype(o_ref.dtype)

    stages = [(functools.partial(scores, j, p, t0), functools.partial(finish, j, p, t0))
              for j in range(tq // blk) for p in range(n_pair) for t0 in range(0, REP_A, nt)]
    _run_pipelined(stages + _memory_stages(qm_ref, kmt_ref, vm_ref, o_ref, 2 * blk), WINDOW_LOOKAHEAD)


def _window_attention(tok, kt, kmt, vm, sink, bias, tq):
    b, t, _ = tok.shape
    blk = WINDOW
    n_blk = t // blk
    per = tq // blk
    assert n_blk >= 2 and t % tq == 0
    vcol = Q_WIDTH // KV_WIDTH_A
    prev_i = lambda i: jnp.maximum(i * per - 1, 0)
    next_i = lambda i: jnp.minimum((i + 1) * per, n_blk - 1)
    kernel = functools.partial(_window_kernel, tq=tq, n_blk=n_blk)
    return pl.pallas_call(
        kernel,
        grid=(b, t // tq),
        in_specs=[
            pl.BlockSpec(memory_space=pltpu.SMEM),
            pl.BlockSpec((1, tq, Q_WIDTH), lambda bi, i: (bi, i, 0)),
            pl.BlockSpec((1, KV_WIDTH_A, blk), lambda bi, i: (bi, 0, prev_i(i))),
            pl.BlockSpec((1, KV_WIDTH_A, tq), lambda bi, i: (bi, 0, i)),
            pl.BlockSpec((1, KV_WIDTH_A, blk), lambda bi, i: (bi, 0, next_i(i))),
            pl.BlockSpec((1, blk, KV_WIDTH_A), lambda bi, i: (bi, prev_i(i), vcol)),
            pl.BlockSpec((1, tq, KV_WIDTH_A), lambda bi, i: (bi, i, vcol)),
            pl.BlockSpec((1, blk, KV_WIDTH_A), lambda bi, i: (bi, next_i(i), vcol)),
            pl.BlockSpec((1, tq, MEM_WIDTH), lambda bi, i: (bi, i, vcol + 1)),
            pl.BlockSpec((1, MEM_WIDTH, MEM_LEN), lambda bi, i: (bi, 0, 0)),
            pl.BlockSpec((1, MEM_LEN, MEM_WIDTH), lambda bi, i: (bi, 0, 0)),
            _resident(bias.shape, lambda bi, i: (0, 0, 0)),
        ],
        out_specs=pl.BlockSpec((1, tq, MIX_WIDTH), lambda bi, i: (bi, i, 0)),
        out_shape=jax.ShapeDtypeStruct((b, t, MIX_WIDTH), BF16),
        scratch_shapes=[pltpu.VMEM((KV_WIDTH_A, tq + 2 * blk), BF16),
                        pltpu.VMEM((tq + 2 * blk, KV_WIDTH_A), BF16)],
        compiler_params=_params("parallel", "parallel"),
        name="window_attention",
    )(sink, tok, kt, kt, kt, tok, tok, tok, tok, kmt, vm, bias)


NB_ROWS_PER_TILE = 4
NB_TQ = NB_ROWS_PER_TILE * GRID_W
NB_TILES_PER_STEP = 4
NB_LOOKAHEAD = 2


def _nbr_bias_table(rpb):
    h, n_dr, n_dc = rpb.shape
    w = GRID_W
    dc = np.arange(w)[None, :] - np.arange(w)[:, None] + NA_COLS - 1
    onehot = (dc[None] == np.arange(n_dc)[:, None, None]).astype(np.float32)
    blocks = jnp.einsum("hrd,dqk->hrqk", rpb.astype(F32), jnp.asarray(onehot),
                        precision=lax.Precision.HIGHEST)
    zero = jnp.zeros((h, w, w), F32)
    rows = []
    for a in range(NB_ROWS_PER_TILE):
        strip = []
        for b in range(3 * NB_ROWS_PER_TILE):
            dr = (b - NB_ROWS_PER_TILE) - a + (NA_ROWS - 1)
            strip.append(blocks[:, dr] if 0 <= dr < n_dr else zero)
        rows.append(jnp.concatenate(strip, axis=-1))
    return jnp.concatenate(rows, axis=1)


def _nbr_mask_table(n_tiles, n_rows):
    qi = np.arange(NB_TQ)[:, None]
    ki = np.arange(3 * NB_TQ)[None, :]
    c0 = np.clip(qi % GRID_W - NA_COLS // 2, 0, GRID_W - NA_COLS)
    k_col = ki % GRID_W
    col_ok = (k_col >= c0) & (k_col < c0 + NA_COLS)
    out = []
    for i in (0, min(1, n_tiles - 1), n_tiles - 1):
        q_row = i * NB_ROWS_PER_TILE + qi // GRID_W
        k_row = (i - 1) * NB_ROWS_PER_TILE + ki // GRID_W
        r0 = np.clip(q_row - NA_ROWS // 2, 0, n_rows - NA_ROWS)
        row_ok = (k_row >= r0) & (k_row < r0 + NA_ROWS)
        out.append(np.where(row_ok & col_ok, np.float32(0.0), np.float32(NEG)))
    return np.stack(out).astype(np.float32)


_NBR_CANONICAL_TILES = 3


def _nbr_logit_table(rpb):
    mask = _nbr_mask_table(_NBR_CANONICAL_TILES, _NBR_CANONICAL_TILES * NB_ROWS_PER_TILE)
    table = (_nbr_bias_table(rpb)[None] + jnp.asarray(mask)[:, None]) * LOG2E
    return table.reshape(3 * N_Q_HEADS, NB_TQ, 3 * NB_TQ)


def _nbr_kernel(q_ref, ktp_ref, ktc_ref, ktn_ref, vp_ref, vc_ref, vn_ref, qm_ref, kmt_ref, vm_ref, table_ref,
                o_ref, *, n_tiles):
    i = pl.program_id(1)
    tq = NB_TQ
    per = NB_TILES_PER_STEP
    low = _low_lanes(tq)
    kt_tiles = ([lambda cols: ktp_ref[0, cols, :]]
                + [(lambda cols, s=s: ktc_ref[0, cols, s * tq:(s + 1) * tq]) for s in range(per)]
                + [lambda cols: ktn_ref[0, cols, :]])
    v_tiles = ([lambda cols: vp_ref[0, :, cols]]
               + [(lambda cols, s=s: vc_ref[0, s * tq:(s + 1) * tq, cols]) for s in range(per)]
               + [lambda cols: vn_ref[0, :, cols]])
    def scores(s, p, hi):
        tile = i * per + s
        variant = jnp.where(tile == 0, 0, jnp.where(tile == n_tiles - 1, 2, 1))
        cols = slice(p * LANES, (p + 1) * LANES)
        qs = _split_heads(q_ref[0, s * tq:(s + 1) * tq, cols], low)[hi]
        kt = jnp.concatenate([f(cols) for f in kt_tiles[s:s + 3]], axis=1)
        sc = jnp.dot(qs, kt, preferred_element_type=F32)
        return sc + table_ref[variant * N_Q_HEADS + 2 * p + hi]

    pending = {}

    def finish(s, p, hi, sc):
        cols = slice(p * LANES, (p + 1) * LANES)
        pr = jnp.exp2(sc - jnp.max(sc, axis=-1, keepdims=True)).astype(BF16)
        v = jnp.concatenate([f(cols) for f in v_tiles[s:s + 3]], axis=0)
        oa = jnp.dot(pr, _with_ones(v), preferred_element_type=F32)
        if not hi:
            pending[(s, p)] = oa
            return
        both = jnp.concatenate([pending.pop((s, p)), oa], axis=0)
        o_ref[0, s * tq:(s + 1) * tq, cols] = (
            _merge_heads(both, low, slice(0, tq), slice(tq, 2 * tq)).astype(o_ref.dtype))

    stages = [(functools.partial(scores, s, p, hi), functools.partial(finish, s, p, hi))
              for s in range(per) for p in range(Q_WIDTH // LANES) for hi in range(2)]
    _run_pipelined(stages + _memory_stages(qm_ref, kmt_ref, vm_ref, o_ref, tq), NB_LOOKAHEAD)


def _nbr_attention(tok, kt, kmt, vm, table):
    b, t, _ = tok.shape
    tq = NB_TQ
    n = t // tq
    n_rows = t // GRID_W
    per = NB_TILES_PER_STEP
    ts = per * tq
    assert n_rows >= NA_ROWS and t % ts == 0 and n >= 2
    canonical = _nbr_mask_table(_NBR_CANONICAL_TILES, _NBR_CANONICAL_TILES * NB_ROWS_PER_TILE)
    assert n == 2 or np.array_equal(_nbr_mask_table(n, n_rows), canonical)
    assert n > 2 or np.array_equal(_nbr_mask_table(n, n_rows)[[0, 2]], canonical[[0, 2]])
    prev_i = lambda i: jnp.maximum(i * per - 1, 0)
    next_i = lambda i: jnp.minimum((i + 1) * per, n - 1)
    return pl.pallas_call(
        functools.partial(_nbr_kernel, n_tiles=n),
        grid=(b, t // ts),
        in_specs=[
            pl.BlockSpec((1, ts, Q_WIDTH), lambda bi, i: (bi, i, 0)),
            pl.BlockSpec((1, Q_WIDTH, tq), lambda bi, i: (bi, 0, prev_i(i))),
            pl.BlockSpec((1, Q_WIDTH, ts), lambda bi, i: (bi, 0, i)),
            pl.BlockSpec((1, Q_WIDTH, tq), lambda bi, i: (bi, 0, next_i(i))),
            pl.BlockSpec((1, tq, Q_WIDTH), lambda bi, i: (bi, prev_i(i), 1)),
            pl.BlockSpec((1, ts, Q_WIDTH), lambda bi, i: (bi, i, 1)),
            pl.BlockSpec((1, tq, Q_WIDTH), lambda bi, i: (bi, next_i(i), 1)),
            pl.BlockSpec((1, ts, MEM_WIDTH), lambda bi, i: (bi, i, 2 * Q_WIDTH // MEM_WIDTH)),
            pl.BlockSpec((1, MEM_WIDTH, MEM_LEN), lambda bi, i: (bi, 0, 0)),
            pl.BlockSpec((1, MEM_LEN, MEM_WIDTH), lambda bi, i: (bi, 0, 0)),
            _resident(table.shape, lambda bi, i: (0, 0, 0)),
        ],
        out_specs=pl.BlockSpec((1, ts, MIX_WIDTH), lambda bi, i: (bi, i, 0)),
        out_shape=jax.ShapeDtypeStruct((b, t, MIX_WIDTH), BF16),
        compiler_params=_params("parallel", "parallel"),
        name="nbr_attention",
    )(tok, kt, kt, kt, tok, tok, tok, tok, kmt, vm, table)


def _post_kernel(xp_ref, xc_ref, xn_ref, op_ref, oc_ref, on_ref, wo_ref, gn_ref, wg_ref, wu_ref, cw_ref, cb_ref,
                 wd_ref, gf_ref, y_ref, oext_ref, hext_ref, act_ref, *, final_norm):
    i = pl.program_id(1)
    n = pl.num_programs(1)
    tm = xc_ref.shape[1]
    ext = tm + HALO
    halo_row = lax.broadcasted_iota(jnp.int32, (HALO, 1), 0)
    take_next = halo_row < HALO // 2
    oext_ref[0:tm] = oc_ref[0]
    oext_ref[tm:] = jnp.where(take_next, on_ref[0], op_ref[0])
    mix = jnp.dot(oext_ref[...], wo_ref[...], preferred_element_type=F32)
    gn = gn_ref[...]
    x1 = xc_ref[0] + mix[0:tm]
    y_ref[0] = x1
    hext_ref[0:tm] = _rms(x1, gn).astype(BF16)
    h_halo = _rms(jnp.where(take_next, xn_ref[0], xp_ref[0]) + mix[tm:], gn).astype(BF16)
    halo_ok = ((halo_row == 0) & (i < n - 1)) | ((halo_row == HALO - 1) & (i > 0))
    hext_ref[tm:] = jnp.where(halo_ok, h_halo, jnp.zeros_like(h_halo))

    for lo, hi in zip(FF_CHUNK_BOUNDS[:-1], FF_CHUNK_BOUNDS[1:]):
        cols = slice(lo, hi)
        g_ext = jnp.dot(hext_ref[...], wg_ref[:, cols], preferred_element_type=F32)
        g = g_ext[0:tm]
        g_prev = pltpu.roll(g_ext, 1, 0)[0:tm]
        g_next = pltpu.roll(g_ext, ext - 1, 0)[0:tm]
        gc = g_prev * cw_ref[0:1, cols] + g * cw_ref[1:2, cols] + g_next * cw_ref[2:3, cols] + cb_ref[:, cols]
        u = jnp.dot(hext_ref[0:tm], wu_ref[:, cols], preferred_element_type=F32)
        act_ref[:, cols] = (gc * jax.nn.sigmoid(gc) * u).astype(BF16)

    y = y_ref[0] + jnp.dot(act_ref[...], wd_ref[...], preferred_element_type=F32)
    if final_norm:
        y = _rms(y, gf_ref[...])
    y_ref[0] = y


def _post_attention(x, o, w_o, ffn, layer, g_final, tm, final_norm):
    b, t, d = x.shape
    per = tm // HALO
    n_halo = t // HALO
    kernel = functools.partial(_post_kernel, final_norm=final_norm)
    const2 = lambda bi, i: (0, 0)
    of_layer = lambda a: pl.BlockSpec((None,) + a.shape[1:], lambda bi, i: (layer, 0, 0),
                                      pipeline_mode=pl.Buffered(1))
    prev_map = lambda bi, i: (bi, jnp.maximum(i * per - 1, 0), 0)
    cur_map = lambda bi, i: (bi, i, 0)
    next_map = lambda bi, i: (bi, jnp.minimum((i + 1) * per, n_halo - 1), 0)
    return pl.pallas_call(
        kernel,
        grid=(b, t // tm),
        in_specs=[
            pl.BlockSpec((1, HALO, d), prev_map),
            pl.BlockSpec((1, tm, d), cur_map),
            pl.BlockSpec((1, HALO, d), next_map),
            pl.BlockSpec((1, HALO, MIX_WIDTH), prev_map),
            pl.BlockSpec((1, tm, MIX_WIDTH), cur_map),
            pl.BlockSpec((1, HALO, MIX_WIDTH), next_map),
            _resident(w_o.shape, const2),
            of_layer(ffn["g_ffn"]),
            of_layer(ffn["w_gate"]),
            of_layer(ffn["w_up"]),
            of_layer(ffn["conv_w"]),
            of_layer(ffn["conv_b"]),
            of_layer(ffn["w_down"]),
            _resident((1, d), const2),
        ],
        out_specs=pl.BlockSpec((1, tm, d), cur_map),
        out_shape=jax.ShapeDtypeStruct((b, t, d), F32),
        scratch_shapes=[pltpu.VMEM((tm + HALO, MIX_WIDTH), BF16), pltpu.VMEM((tm + HALO, d), BF16),
                        pltpu.VMEM((tm, D_FF), BF16)],
        compiler_params=_params("parallel", "parallel"),
        name="post_attention",
    )(x, x, x, o, o, o, w_o, ffn["g_ffn"], ffn["w_gate"], ffn["w_up"], ffn["conv_w"], ffn["conv_b"], ffn["w_down"],
      g_final.reshape(1, d))


def _prepare_weights(g_mix, g_mem, w_in_a, sink_a, w_in_b, rpb_b, w_mem_kv, w_o, g_ffn, w_gate, w_up, conv_w,
                     conv_b, w_down, g_final):
    head_cols = np.arange(Q_WIDTH).reshape(N_Q_HEADS, HEAD_DIM)
    perm_a = head_cols[_window_head_order()].reshape(-1)
    layers = []
    for i in range(DEPTH):
        j = i // 2
        wo = w_o[i]
        if i % 2 == 0:
            w = w_in_a[j]
            wq, wk, wv, wqm = (w[:, :Q_WIDTH], w[:, Q_WIDTH:Q_WIDTH + KV_WIDTH_A],
                               w[:, Q_WIDTH + KV_WIDTH_A:Q_WIDTH + 2 * KV_WIDTH_A], w[:, Q_WIDTH + 2 * KV_WIDTH_A:])
            wq = wq[:, perm_a]
            wo = jnp.concatenate([wo[:Q_WIDTH][perm_a], wo[Q_WIDTH:]], axis=0)
            extra = dict(sink=sink_a[j].astype(F32) * LOG2E, bias=jnp.asarray(_window_bias_table()))
        else:
            w = w_in_b[j]
            wq, wk, wv, wqm = (w[:, :Q_WIDTH], w[:, Q_WIDTH:2 * Q_WIDTH], w[:, 2 * Q_WIDTH:3 * Q_WIDTH],
                               w[:, 3 * Q_WIDTH:])
            extra = dict(bias=_nbr_logit_table(rpb_b[j]))
        tok_scale = np.concatenate([np.full(wq.shape[1], Q_SCALE), np.ones(wv.shape[1]),
                                    np.full(wqm.shape[1], Q_SCALE)]).astype(np.float32)
        layers.append(dict(
            g_mix=g_mix[i],
            w_tok=jnp.concatenate([wq, wv, wqm], axis=1).astype(BF16), tok_scale=jnp.asarray(tok_scale),
            w_kt=wk.T.astype(BF16),
            w_o=wo.astype(BF16), **extra))
    ffn = dict(g_ffn=g_ffn.reshape(DEPTH, 1, D_MODEL), w_gate=w_gate.astype(BF16), w_up=w_up.astype(BF16),
               conv_w=conv_w, conv_b=conv_b.reshape(DEPTH, 1, D_FF), w_down=w_down.astype(BF16))
    mem_weights = dict(g_mem=g_mem, w_vm=w_mem_kv[:, :, MEM_WIDTH:].astype(BF16),
                       w_kmt=jnp.swapaxes(w_mem_kv[:, :, :MEM_WIDTH], 1, 2).astype(BF16))
    return layers, ffn, mem_weights


def _trunk(x, mem, layers, ffn, mem_weights, g_final):
    b, t, d = x.shape
    n = b * t
    assert t % PROJ_ROWS == 0 and t % WINDOW_ROWS == 0 and t % POST_ROWS == 0
    mem_kv = _memory_project(mem, mem_weights["g_mem"], mem_weights["w_vm"], mem_weights["w_kmt"])
    for i, lw in enumerate(layers):
        tok, kt = _norm_project(x.reshape(n, d), lw["g_mix"], lw["w_tok"], lw["tok_scale"], lw["w_kt"], b, PROJ_ROWS)
        tok = tok.reshape(b, t, -1)
        vm, kmt = mem_kv[i]
        if i % 2 == 0:
            o = _window_attention(tok, kt, kmt, vm, lw["sink"], lw["bias"], WINDOW_ROWS)
        else:
            o = _nbr_attention(tok, kt, kmt, vm, lw["bias"])
        x = _post_attention(x, o, lw["w_o"], ffn, i, g_final, POST_ROWS, final_norm=(i == DEPTH - 1))
    return x


def kernel(x_prompt, x_sample, mem_prompt, mem_sample, g_mix, g_mem, w_in_a, sink_a, w_in_b, rpb_b, w_mem_kv, w_o,
           g_ffn, w_gate, w_up, conv_w, conv_b, w_down, g_final):
    layers, ffn, mem_weights = _prepare_weights(g_mix, g_mem, w_in_a, sink_a, w_in_b, rpb_b, w_mem_kv, w_o, g_ffn,
                                                w_gate, w_up, conv_w, conv_b, w_down, g_final)
    y_prompt = _trunk(x_prompt, mem_prompt, layers, ffn, mem_weights, g_final)
    y_sample = _trunk(x_sample, mem_sample, layers, ffn, mem_weights, g_final)
    return (y_prompt, y_sample)
```
